```python
import math
import jax, jax.numpy as jnp
from jax import lax
import numpy as np

D_MODEL = 1024
BATCH = 8
SEQ = 4096
DEPTH = 1
DEC_BATCH = 32
DEC_SEQ = 4
PAST_LEN = 16384
PAGE_SIZE = 128

MIX = D_MODEL
DA_HEADS = 4
DA_QK = 64
DA_V = 2 * DA_QK
DA_W = DA_HEADS * DA_V
HG_HEADS = 4
HG_K = 128
HG_V = 128
HG_W = HG_HEADS * HG_V
HG_CHUNK = 16
N_BUCKETS = 32
MAX_DISTANCE = 128
Q_BLOCK = 128
N_GROUPS = 4
EXPERTS_PER_GROUP = 4
N_EXPERTS = N_GROUPS * EXPERTS_PER_GROUP
D_EXPERT = 256
EPS = 1e-6
SPLITS = (DA_HEADS * 2 * DA_QK, DA_HEADS * 2 * DA_QK, DA_W,
          HG_HEADS * HG_K, HG_HEADS * HG_K, HG_W, HG_W)
IN_TOTAL = sum(SPLITS)

kernel_name = 'hybrid_diffattn_hgrn2_hmoe_step'


def rmsnorm(x, w):
    xf = x.astype(jnp.float32)
    y = xf * lax.rsqrt(jnp.mean(xf * xf, axis=-1, keepdims=True) + EPS)
    return (y * w.astype(jnp.float32)).astype(x.dtype)


def split_proj(p):
    offs = np.cumsum(SPLITS)[:-1].tolist()
    return jnp.split(p, offs, axis=-1)


def t5_bucket(n):
    n = jnp.maximum(n, 0)
    max_exact = N_BUCKETS // 2
    nf = jnp.maximum(n, max_exact).astype(jnp.float32)
    large = max_exact + (jnp.log(nf / max_exact) / math.log(MAX_DISTANCE / max_exact)
                         * (N_BUCKETS - max_exact)).astype(jnp.int32)
    large = jnp.minimum(large, N_BUCKETS - 1)
    return jnp.where(n < max_exact, n, large)


def diff_attend(q, segments, q_pos, rel_bias, lam):
    scale = DA_QK ** -0.5
    logits = []
    for k, _, k_pos in segments:
        s = jnp.einsum('bqhmd,bkhmd->bmhqk', q, k).astype(jnp.float32) * scale
        n = q_pos[:, None] - k_pos[None, :]
        bias = jnp.transpose(rel_bias[t5_bucket(n)], (2, 0, 1)).astype(jnp.float32)
        logits.append(jnp.where(n >= 0, s + bias, -jnp.inf))
    p = jax.nn.softmax(jnp.concatenate(logits, axis=-1), axis=-1)
    attn = p[:, 0] - lam * p[:, 1]
    out = 0.0
    off = 0
    for k, v, k_pos in segments:
        kl = k_pos.shape[0]
        out = out + jnp.einsum('bhqk,bkhv->bqhv', attn[..., off:off + kl], v)
        off += kl
    return out.astype(jnp.float32)


def diff_attn_prompt(q, k, v, rel_bias, lam):
    b, s = q.shape[:2]
    k_pos = jnp.arange(s, dtype=jnp.int32)

    def block(i):
        start = i * Q_BLOCK
        qb = lax.dynamic_slice_in_dim(q, start, Q_BLOCK, axis=1)
        q_pos = start + jnp.arange(Q_BLOCK, dtype=jnp.int32)
        return diff_attend(qb, [(k, v, k_pos)], q_pos, rel_bias, lam)

    out = lax.map(block, jnp.arange(s // Q_BLOCK, dtype=jnp.int32))
    return jnp.moveaxis(out, 0, 1).reshape(b, s, DA_HEADS, DA_V)


def hgrn2_scan(q, k, v, g, s0, chunk):
    b, l, h, _ = q.shape
    n = l // chunk

    def to_chunks(t):
        return jnp.moveaxis(t.reshape((b, n, chunk) + t.shape[2:]), 1, 0)

    causal = jnp.tril(jnp.ones((chunk, chunk), dtype=bool))[None, :, :, None, None]

    def step(S, inp):
        qc, kc, vc, gc = inp
        G = jnp.cumsum(gc, axis=1)
        o_inter = jnp.einsum('bthk,bhkv->bthv', qc * jnp.exp(G), S)
        decay = jnp.exp(jnp.where(causal, G[:, :, None] - G[:, None, :], -jnp.inf))
        A = jnp.einsum('bthk,bshk,btshk->bhts', qc, kc, decay)
        o_intra = jnp.einsum('bhts,bshv->bthv', A, vc)
        G_last = G[:, -1]
        S_new = (jnp.exp(G_last)[..., None] * S
                 + jnp.einsum('bshk,bshv->bhkv', kc * jnp.exp(G_last[:, None] - G), vc))
        return S_new, o_inter + o_intra

    S_fin, o = lax.scan(step, s0, (to_chunks(q), to_chunks(k), to_chunks(v), to_chunks(g)))
    return jnp.moveaxis(o, 0, 1).reshape(b, l, h, v.shape[-1]), S_fin


def hgrn2_mixer(hq, hf, hi, hgate, lb, s0, chunk, norm_w):
    b, l = hq.shape[:2]
    f32 = jnp.float32
    q = jax.nn.silu(hq.astype(f32)).reshape(b, l, HG_HEADS, HG_K) * (HG_K ** -0.5)
    f = lb + (1.0 - lb) * jax.nn.sigmoid(hf.astype(f32))
    g = jnp.log(f).reshape(b, l, HG_HEADS, HG_K)
    k = (1.0 - f).reshape(b, l, HG_HEADS, HG_K)
    v = hi.astype(f32).reshape(b, l, HG_HEADS, HG_V)
    o, s_fin = hgrn2_scan(q, k, v, g, s0.astype(f32), chunk)
    o = rmsnorm(o, norm_w) * jax.nn.silu(hgate.astype(f32).reshape(b, l, HG_HEADS, HG_V))
    return o.reshape(b, l, HG_W), s_fin


def hier_moe(h, wg, bg, we, be, w1, w3, w2):
    shp = h.shape
    t = h.reshape(-1, shp[-1])
    f32 = jnp.float32
    pg = jax.nn.softmax((t @ wg).astype(f32) + bg.astype(f32), axis=-1)
    p_top, g_idx = lax.top_k(pg, 1)
    le = ((t @ we).astype(f32) + be.astype(f32)).reshape(-1, N_GROUPS, EXPERTS_PER_GROUP)
    le_sel = jnp.take_along_axis(le, g_idx[:, :, None], axis=1)[:, 0]
    top_l, e_idx = lax.top_k(le_sel, 2)
    gate = p_top * jax.nn.softmax(top_l, axis=-1)
    expert_id = g_idx * EXPERTS_PER_GROUP + e_idx
    combine = jnp.sum(jax.nn.one_hot(expert_id, N_EXPERTS, dtype=f32) * gate[..., None], axis=1)
    y = jnp.zeros(t.shape, f32)
    for e in range(N_EXPERTS):
        he = jax.nn.silu(t @ w1[e]) * (t @ w3[e])
        y = y + combine[:, e:e + 1] * (he @ w2[e])
    return y.astype(h.dtype).reshape(shp)


def setup_inputs(seed: int = 0) -> dict:
    key = jax.random.key(seed)
    ks = jax.random.split(key, 24)
    f32 = jnp.float32
    n_pages = PAST_LEN // PAGE_SIZE
    n_used = DEC_BATCH * n_pages
    n_pool = n_used + n_used // 4
    nrm = lambda k, shape, s: jax.random.normal(k, shape, f32) * s
    return {
        'x_prompt': nrm(ks[0], (BATCH, SEQ, D_MODEL), 1.0),
        'x_sample': nrm(ks[1], (DEC_BATCH, DEC_SEQ, D_MODEL), 1.0),
        'cache_k': nrm(ks[2], (DEPTH, n_pool, PAGE_SIZE, DA_HEADS, 2 * DA_QK), 1.0),
        'cache_v': nrm(ks[3], (DEPTH, n_pool, PAGE_SIZE, DA_HEADS, DA_V), 1.0),
        'state_hgrn': nrm(ks[4], (DEPTH, DEC_BATCH, HG_HEADS, HG_K, HG_V), 0.3),
        'page_table': jax.random.permutation(ks[5], n_pool)[:n_used].reshape(DEC_BATCH, n_pages).astype(jnp.int32),
        'norm1_w': 1.0 + nrm(ks[6], (DEPTH, D_MODEL), 0.01),
        'w_in': nrm(ks[7], (DEPTH, D_MODEL, IN_TOTAL), D_MODEL ** -0.5),
        'rel_bias': nrm(ks[8], (N_BUCKETS, DA_HEADS), 0.5),
        'lambda_q1': nrm(ks[9], (DEPTH, DA_QK), 0.1),
        'lambda_k1': nrm(ks[10], (DEPTH, DA_QK), 0.1),
        'lambda_q2': nrm(ks[11], (DEPTH, DA_QK), 0.1),
        'lambda_k2': nrm(ks[12], (DEPTH, DA_QK), 0.1),
        'da_subln_w': 1.0 + nrm(ks[13], (DEPTH, DA_V), 0.01),
        'hg_lb_logits': nrm(ks[14], (DEPTH + 1, HG_HEADS * HG_K), 0.1),
        'hg_norm_w': 1.0 + nrm(ks[15], (DEPTH, HG_V), 0.01),
        'w_out': nrm(ks[16], (DEPTH, MIX, D_MODEL), MIX ** -0.5),
        'norm2_w': 1.0 + nrm(ks[17], (DEPTH, D_MODEL), 0.01),
        'router_g_w': nrm(ks[18], (DEPTH, D_MODEL, N_GROUPS), D_MODEL ** -0.5),
        'router_g_b': nrm(ks[19], (DEPTH, N_GROUPS), 0.01),
        'router_e_w': nrm(ks[20], (DEPTH, D_MODEL, N_EXPERTS), D_MODEL ** -0.5),
        'router_e_b': nrm(ks[21], (DEPTH, N_EXPERTS), 0.01),
        'expert_w1': nrm(ks[22], (DEPTH, N_EXPERTS, D_MODEL, D_EXPERT), D_MODEL ** -0.5),
        'expert_w3': nrm(jax.random.fold_in(ks[22], 1), (DEPTH, N_EXPERTS, D_MODEL, D_EXPERT), D_MODEL ** -0.5),
        'expert_w2': nrm(ks[23], (DEPTH, N_EXPERTS, D_EXPERT, D_MODEL), D_EXPERT ** -0.5),
        'final_norm_w': 1.0 + nrm(jax.random.fold_in(ks[23], 1), (D_MODEL,), 0.01),
    }


def reference(x_prompt, x_sample, cache_k, cache_v, state_hgrn, page_table,
              norm1_w, w_in, rel_bias, lambda_q1, lambda_k1, lambda_q2, lambda_k2,
              da_subln_w, hg_lb_logits, hg_norm_w, w_out, norm2_w,
              router_g_w, router_g_b, router_e_w, router_e_b,
              expert_w1, expert_w3, expert_w2, final_norm_w):
    f32 = jnp.float32
    lb_all = jnp.cumsum(jax.nn.softmax(hg_lb_logits.astype(f32), axis=0), axis=0)
    past = page_table.shape[1] * PAGE_SIZE

    def attend_prompt(q, k, v, lam, l):
        return diff_attn_prompt(q, k, v, rel_bias, lam)

    def attend_sample(q, k, v, lam, l):
        bd, t = q.shape[:2]
        kp = cache_k[l][page_table].reshape(bd, past, DA_HEADS, 2, DA_QK)
        vp = cache_v[l][page_table].reshape(bd, past, DA_HEADS, DA_V)
        past_pos = jnp.arange(past, dtype=jnp.int32)
        new_pos = past + jnp.arange(t, dtype=jnp.int32)
        return diff_attend(q, [(kp, vp, past_pos), (k, v, new_pos)], new_pos, rel_bias, lam)

    def layer(x, l, attend, s0, chunk):
        b, s = x.shape[:2]
        lam_init = 0.8 - 0.6 * math.exp(-0.3 * l)
        lam = (jnp.exp(jnp.sum(lambda_q1[l].astype(f32) * lambda_k1[l].astype(f32)))
               - jnp.exp(jnp.sum(lambda_q2[l].astype(f32) * lambda_k2[l].astype(f32))) + lam_init)
        h = rmsnorm(x, norm1_w[l])
        dq, dk, dv, hq, hf, hi, hgate = split_proj(h @ w_in[l])
        q = dq.reshape(b, s, DA_HEADS, 2, DA_QK)
        k = dk.reshape(b, s, DA_HEADS, 2, DA_QK)
        v = dv.reshape(b, s, DA_HEADS, DA_V)
        o_da = rmsnorm(attend(q, k, v, lam, l), da_subln_w[l]) * (1.0 - lam_init)
        o_hg, s_new = hgrn2_mixer(hq, hf, hi, hgate, lb_all[l], s0, chunk, hg_norm_w[l])
        mixed = jnp.concatenate([o_da.reshape(b, s, DA_W), o_hg], axis=-1).astype(x.dtype)
        x = x + mixed @ w_out[l]
        x = x + hier_moe(rmsnorm(x, norm2_w[l]), router_g_w[l], router_g_b[l], router_e_w[l],
                         router_e_b[l], expert_w1[l], expert_w3[l], expert_w2[l])
        return x, k.reshape(b, s, DA_HEADS, 2 * DA_QK), v, s_new

    xp, xs = x_prompt, x_sample
    kp_l, vp_l, sp_l, ks_l, vs_l, ss_l = [], [], [], [], [], []
    for l in range(DEPTH):
        s0p = jnp.zeros((xp.shape[0], HG_HEADS, HG_K, HG_V), f32)
        xp, kp, vp, sp = layer(xp, l, attend_prompt, s0p, HG_CHUNK)
        xs, ksm, vsm, ssm = layer(xs, l, attend_sample, state_hgrn[l], xs.shape[1])
        kp_l.append(kp); vp_l.append(vp); sp_l.append(sp.astype(xp.dtype))
        ks_l.append(ksm); vs_l.append(vsm); ss_l.append(ssm.astype(state_hgrn.dtype))
    y_prompt = rmsnorm(xp, final_norm_w)
    y_sample = rmsnorm(xs, final_norm_w)
    return (y_prompt, y_sample, jnp.stack(kp_l), jnp.stack(vp_l), jnp.stack(sp_l),
            jnp.stack(ks_l), jnp.stack(vs_l), jnp.stack(ss_l))
```

```python
import functools
import math

import jax
import jax.numpy as jnp
from jax import lax
from jax.experimental import pallas as pl
from jax.experimental.pallas import tpu as pltpu

F32 = jnp.float32
BF16 = jnp.bfloat16

D_MODEL = 1024
PAGE_SIZE = 128
DA_HEADS = 4
DA_QK = 64
DA_V = 128
DA_W = DA_HEADS * DA_V
HG_HEADS = 4
HG_K = 128
HG_V = 128
HG_W = HG_HEADS * HG_V
N_BUCKETS = 32
MAX_DISTANCE = 128
N_GROUPS = 4
EXPERTS_PER_GROUP = 4
N_EXPERTS = N_GROUPS * EXPERTS_PER_GROUP
D_EXPERT = 256
EPS = 1e-6
IN_TOTAL = 7 * 512
LAM_INIT = 0.8 - 0.6 * math.exp(-0.3 * 0)

NEG_BIG = -1e30
V7X_LANES = 128
VMEM_LIMIT = 56 * 1024 * 1024

ATT_BLOCK = 512
HG_CHUNK = 128
HG_DIAG = 16
SAMPLE_CHUNK = 16
PAGES_PER_STEP = 8


def _cparams(*sem):
    return pltpu.CompilerParams(dimension_semantics=sem, vmem_limit_bytes=VMEM_LIMIT)


def _dot(a, b):
    return jnp.dot(a, b, preferred_element_type=F32)


def _dot_nt(a, b):
    return lax.dot_general(a, b, (((1,), (1,)), ((), ())), preferred_element_type=F32)


def _dot_tn(a, b):
    return lax.dot_general(a, b, (((0,), (0,)), ((), ())), preferred_element_type=F32)


def _rms(x, w):
    return x * lax.rsqrt(jnp.mean(x * x, axis=-1, keepdims=True) + EPS) * w


def _bias_body(rb_ref, n_ref, o_ref):
    h = pl.program_id(0)
    n = n_ref[...]
    nn = jnp.maximum(n, 0)
    max_exact = N_BUCKETS // 2
    nf = jnp.maximum(nn, max_exact).astype(F32)
    large = max_exact + (jnp.log(nf / max_exact) / math.log(MAX_DISTANCE / max_exact)
                         * (N_BUCKETS - max_exact)).astype(jnp.int32)
    large = jnp.minimum(large, N_BUCKETS - 1)
    bucket = jnp.where(nn < max_exact, nn, large)
    bias = jnp.zeros(n.shape, F32)
    for b in range(N_BUCKETS):
        bias = jnp.where(bucket == b, rb_ref[b, h], bias)
    o_ref[0] = jnp.where(n >= 0, bias, NEG_BIG)


def _bias_tiles(rel_bias, n):
    r, c = n.shape
    return pl.pallas_call(
        _bias_body,
        grid=(DA_HEADS,),
        in_specs=[pl.BlockSpec(memory_space=pltpu.SMEM),
                  pl.BlockSpec((r, c), lambda h: (0, 0))],
        out_specs=pl.BlockSpec((1, r, c), lambda h: (h, 0, 0)),
        out_shape=jax.ShapeDtypeStruct((DA_HEADS, r, c), F32),
        compiler_params=_cparams("arbitrary"),
        name="bias_tiles",
    )(rel_bias, n)


def _proj_body(x_ref, nw_ref, w_ref, q_ref, k_ref, v_ref, kb_ref, vb_ref, hg_ref):
    hb = _rms(x_ref[...], nw_ref[...]).astype(BF16)
    q_ref[...] = (_dot(hb, w_ref[:, 0:512]) * (DA_QK ** -0.5)).astype(BF16)
    k = _dot(hb, w_ref[:, 512:1024])
    k_ref[...] = k
    kb_ref[...] = k.astype(BF16)
    v = _dot(hb, w_ref[:, 1024:1536])
    v_ref[...] = v
    vb_ref[...] = v.astype(BF16)
    hg_ref[...] = _dot(hb, w_ref[:, 1536:IN_TOTAL])


def _proj(x, norm_w, w_in_b, tm):
    t = x.shape[0]
    row = lambda i: (i, 0)
    const = lambda i: (0, 0)
    return pl.pallas_call(
        _proj_body,
        grid=(t // tm,),
        in_specs=[pl.BlockSpec((tm, D_MODEL), row),
                  pl.BlockSpec((1, D_MODEL), const),
                  pl.BlockSpec((D_MODEL, IN_TOTAL), const)],
        out_specs=[pl.BlockSpec((tm, 512), row)] * 5 + [pl.BlockSpec((tm, 2048), row)],
        out_shape=[jax.ShapeDtypeStruct((t, 512), BF16),
                   jax.ShapeDtypeStruct((t, 512), F32),
                   jax.ShapeDtypeStruct((t, 512), F32),
                   jax.ShapeDtypeStruct((t, 512), BF16),
                   jax.ShapeDtypeStruct((t, 512), BF16),
                   jax.ShapeDtypeStruct((t, 2048), F32)],
        compiler_params=_cparams("arbitrary"),
        name="proj",
    )(x, norm_w, w_in_b)


def _softmax_update(s, v, m_ref, l_ref, acc_ref, idx):
    m_prev = m_ref[idx]
    m_new = jnp.maximum(m_prev, jnp.max(s, axis=-1, keepdims=True))
    alpha = jnp.exp(m_prev - m_new)
    p = jnp.exp(s - m_new)
    l_ref[idx] = alpha * l_ref[idx] + jnp.sum(p, axis=-1, keepdims=True)
    acc_ref[idx] = alpha * acc_ref[idx] + _dot(p.astype(BF16), v)
    m_ref[idx] = m_new


def _sub_ln(o, w):
    return _rms(o, w) * (1.0 - LAM_INIT)


def _attn_prompt_body(lam_ref, q_ref, k_ref, v_ref, bias_ref, w_ref, o_ref,
                      m_ref, l_ref, acc_ref):
    qi = pl.program_id(2)
    tb = ATT_BLOCK
    q = q_ref[...]
    lane = lax.broadcasted_iota(jnp.int32, q.shape, 1)
    zero = jnp.zeros_like(q)
    qm = (jnp.where(lane < DA_QK, q, zero), jnp.where(lane >= DA_QK, q, zero))

    m_ref[...] = jnp.full(m_ref.shape, NEG_BIG, F32)
    l_ref[...] = jnp.zeros(l_ref.shape, F32)
    acc_ref[...] = jnp.zeros(acc_ref.shape, F32)

    def block(kj, bias):
        start = pl.multiple_of(kj * tb, tb)
        kb = k_ref[pl.ds(start, tb), :]
        vb = v_ref[pl.ds(start, tb), :]
        for mp in range(2):
            _softmax_update(_dot_nt(qm[mp], kb) + bias, vb, m_ref, l_ref, acc_ref, mp)

    far = bias_ref[0, tb:tb + 8, 0:V7X_LANES][0:1, 0:1]

    def far_body(kj, carry):
        block(kj, far)
        return carry

    lax.fori_loop(0, jnp.maximum(qi - 1, 0), far_body, 0)

    @pl.when(qi >= 1)
    def _():
        block(qi - 1, bias_ref[0, tb:2 * tb, :])

    block(qi, bias_ref[0, 0:tb, :])

    o = acc_ref[0] / l_ref[0] - lam_ref[0] * (acc_ref[1] / l_ref[1])
    o_ref[...] = _sub_ln(o, w_ref[...]).astype(BF16)


def _attn_prompt(lam, qb, kb, vb, bias, subln_w, batch, seq):
    tb = ATT_BLOCK
    nq = seq // tb
    return pl.pallas_call(
        _attn_prompt_body,
        grid=(batch, DA_HEADS, nq),
        in_specs=[pl.BlockSpec(memory_space=pltpu.SMEM),
                  pl.BlockSpec((tb, DA_V), lambda b, h, i: (b * nq + i, h)),
                  pl.BlockSpec((seq, DA_V), lambda b, h, i: (b, h)),
                  pl.BlockSpec((seq, DA_V), lambda b, h, i: (b, h)),
                  pl.BlockSpec((1, 2 * tb, tb), lambda b, h, i: (h, 0, 0)),
                  pl.BlockSpec((1, DA_V), lambda b, h, i: (0, 0))],
        out_specs=pl.BlockSpec((tb, DA_V), lambda b, h, i: (b * nq + i, h)),
        out_shape=jax.ShapeDtypeStruct((batch * seq, DA_W), BF16),
        scratch_shapes=[pltpu.VMEM((2, tb, 1), F32),
                        pltpu.VMEM((2, tb, 1), F32),
                        pltpu.VMEM((2, tb, DA_V), F32)],
        compiler_params=_cparams("arbitrary", "arbitrary", "arbitrary"),
        name="attn_prompt",
    )(lam, qb, kb, vb, bias, subln_w)


def _attn_sample_body(pt_ref, lam_ref, q_ref, bias_ref, kn_ref, vn_ref, w_ref, *rest):
    npg = PAGES_PER_STEP
    k_refs = rest[:npg]
    v_refs = rest[npg:2 * npg]
    o_ref = rest[2 * npg]
    m_ref, l_ref, acc_ref = rest[2 * npg + 1:]
    j = pl.program_id(1)
    last = pl.num_programs(1) - 1
    rows = DA_HEADS * 8

    @pl.when(j == 0)
    def _():
        m_ref[...] = jnp.full(m_ref.shape, NEG_BIG, F32)
        l_ref[...] = jnp.zeros(l_ref.shape, F32)
        acc_ref[...] = jnp.zeros(acc_ref.shape, F32)

    q = q_ref[0]
    lane = lax.broadcasted_iota(jnp.int32, q.shape, 1)
    row = lax.broadcasted_iota(jnp.int32, q.shape, 0)
    w = jnp.where(lane // DA_QK == row // 4, q, jnp.zeros_like(q))

    far = bias_ref[0]
    near = bias_ref[1]
    s_parts = []
    for i in range(npg):
        kp = k_refs[i][0].astype(BF16)
        bias = far if i < npg - 1 else jnp.where(j == last, near, far)
        s_parts.append(_dot_nt(w, kp) + bias)
    s = jnp.concatenate(s_parts, axis=1)
    vcat = jnp.concatenate([v_refs[i][0].astype(BF16) for i in range(npg)], axis=0)
    _softmax_update(s, vcat, m_ref, l_ref, acc_ref, 0)

    @pl.when(j == last)
    def _():
        s_new = _dot_nt(w, kn_ref[0]) + bias_ref[2][:, 0:8]
        _softmax_update(s_new, vn_ref[0], m_ref, l_ref, acc_ref, 0)
        normed = acc_ref[0] / l_ref[0]
        for h in range(DA_HEADS):
            blk = normed[h * 8:(h + 1) * 8, h * DA_V:(h + 1) * DA_V]
            o = blk[0:4] - lam_ref[0] * blk[4:8]
            o_ref[0, :, h * DA_V:(h + 1) * DA_V] = _sub_ln(o, w_ref[...]).astype(BF16)
    del rows


def _attn_sample(page_table, lam, q_rep, bias, k_new, v_new, subln_w, cache_k, cache_v):
    nb, n_pages = page_table.shape
    npg = PAGES_PER_STEP
    rows = DA_HEADS * 8

    def page_spec(i):
        return pl.BlockSpec((1, PAGE_SIZE, DA_W),
                            lambda b, j, pt: (pt[b, j * npg + i], 0, 0))

    seq_spec = lambda shape: pl.BlockSpec(shape, lambda b, j, pt: (b, 0, 0))
    grid_spec = pltpu.PrefetchScalarGridSpec(
        num_scalar_prefetch=1,
        grid=(nb, n_pages // npg),
        in_specs=[pl.BlockSpec(memory_space=pltpu.SMEM),
                  seq_spec((1, rows, DA_W)),
                  pl.BlockSpec((3, rows, V7X_LANES), lambda b, j, pt: (0, 0, 0)),
                  seq_spec((1, 8, DA_W)),
                  seq_spec((1, 8, DA_W)),
                  pl.BlockSpec((1, DA_V), lambda b, j, pt: (0, 0))]
                 + [page_spec(i) for i in range(npg)] * 2,
        out_specs=seq_spec((1, 4, DA_W)),
        scratch_shapes=[pltpu.VMEM((1, rows, 1), F32),
                        pltpu.VMEM((1, rows, 1), F32),
                        pltpu.VMEM((1, rows, DA_W), F32)],
    )
    return pl.pallas_call(
        _attn_sample_body,
        grid_spec=grid_spec,
        out_shape=jax.ShapeDtypeStruct((nb, 4, DA_W), BF16),
        compiler_params=_cparams("arbitrary", "arbitrary"),
        name="attn_sample",
    )(page_table, lam, q_rep, bias, k_new, v_new, subln_w,
      *([cache_k] * npg), *([cache_v] * npg))


def _split3(x):
    hi = x.astype(BF16)
    r = x - hi.astype(F32)
    mid = r.astype(BF16)
    lo = (r - mid.astype(F32)).astype(BF16)
    return jnp.concatenate([hi, mid, lo], axis=1)


def _merge3(p):
    w = p.shape[1] // 3
    return p[:, 0:w] + p[:, w:2 * w] + p[:, 2 * w:3 * w]


def _hgrn_head(hq, hf, hi, hgate, lb, norm_w, st, chunk, diag, valid):
    c = chunk
    row = lax.broadcasted_iota(jnp.int32, (c, HG_K), 0)
    q = hq * jax.nn.sigmoid(hq) * (HG_K ** -0.5)
    f = lb + (1.0 - lb) * jax.nn.sigmoid(hf)
    if valid < c:
        f = jnp.where(row < valid, f, 1.0)
        q = jnp.where(row < valid, q, 0.0)
    g = jnp.log(f)
    kk = 1.0 - f
    vb = hi.astype(BF16)

    r2 = lax.broadcasted_iota(jnp.int32, (c, c), 0)
    c2 = lax.broadcasted_iota(jnp.int32, (c, c), 1)
    tri = jnp.where(c2 <= r2, 1.0, 0.0).astype(BF16)
    gc = _merge3(_dot(tri, _split3(g)))

    a = jnp.zeros((c, c), F32)
    rmod = row % diag
    for d in range(diag):
        if d == 0:
            x = q * kk
        else:
            e = jnp.where(rmod >= d, gc - pltpu.roll(gc, d, axis=0), NEG_BIG)
            x = q * pltpu.roll(kk, d, axis=0) * jnp.exp(e)
        a = a + jnp.where(r2 - c2 == d, jnp.sum(x, axis=-1, keepdims=True), 0.0)

    levels = []
    half = diag
    while half < c:
        levels.append(half)
        half *= 2
    if levels:
        gsplit = _split3(gc)
        for half in levels:
            sel = jnp.where(c2 == (r2 // (2 * half)) * (2 * half) + half - 1, 1.0, 0.0).astype(BF16)
            bnd = _merge3(_dot(sel, gsplit))
            second = (row % (2 * half)) >= half
            qh = q * jnp.exp(jnp.where(second, gc - bnd, NEG_BIG))
            kh = kk * jnp.exp(jnp.where(second, NEG_BIG, bnd - gc))
            al = _dot_nt(qh.astype(BF16), kh.astype(BF16))
            a = a + jnp.where(r2 // (2 * half) == c2 // (2 * half), al, 0.0)

    g_last = gc[c - 1:c, :]
    o = _dot_nt((q * jnp.exp(gc)).astype(BF16), st.astype(BF16)) + _dot(a.astype(BF16), vb)
    kl = (kk * jnp.exp(g_last - gc)).astype(BF16)
    st_new = st * jnp.exp(g_last) + _dot_tn(vb, kl)
    o = _rms(o, norm_w) * (hgate * jax.nn.sigmoid(hgate))
    return o, st_new


def _hgrn_body(hg_ref, lb_ref, nw_ref, s0_ref, o_ref, sf_ref, st_ref, *, chunk, diag, valid):
    ci = pl.program_id(1)

    @pl.when(ci == 0)
    def _():
        for h in range(HG_HEADS):
            st_ref[h] = s0_ref[0, h].T

    for h in range(HG_HEADS):
        sl = lambda part: hg_ref[:, part * HG_W + h * HG_K: part * HG_W + (h + 1) * HG_K]
        o, st_new = _hgrn_head(sl(0), sl(1), sl(2), sl(3),
                               lb_ref[:, h * HG_K:(h + 1) * HG_K], nw_ref[...],
                               st_ref[h], chunk, diag, valid)
        st_ref[h] = st_new
        o_ref[:, h * HG_V:(h + 1) * HG_V] = o.astype(BF16)

    @pl.when(ci == pl.num_programs(1) - 1)
    def _():
        for h in range(HG_HEADS):
            sf_ref[0, h] = st_ref[h].T


def _hgrn(hg, lb, norm_w, s0, batch, length, chunk, diag, valid):
    nc = length // chunk
    body = functools.partial(_hgrn_body, chunk=chunk, diag=diag, valid=valid)
    state_spec = pl.BlockSpec((1, HG_HEADS, HG_K, HG_V), lambda b, c: (b, 0, 0, 0))
    return pl.pallas_call(
        body,
        grid=(batch, nc),
        in_specs=[pl.BlockSpec((chunk, 4 * HG_W), lambda b, c: (b * nc + c, 0)),
                  pl.BlockSpec((1, HG_W), lambda b, c: (0, 0)),
                  pl.BlockSpec((1, HG_V), lambda b, c: (0, 0)),
                  state_spec],
        out_specs=[pl.BlockSpec((chunk, HG_W), lambda b, c: (b * nc + c, 0)), state_spec],
        out_shape=[jax.ShapeDtypeStruct((batch * length, HG_W), BF16),
                   jax.ShapeDtypeStruct((batch, HG_HEADS, HG_K, HG_V), F32)],
        scratch_shapes=[pltpu.VMEM((HG_HEADS, HG_V, HG_K), F32)],
        compiler_params=_cparams("arbitrary", "arbitrary"),
        name="hgrn",
    )(hg, lb, norm_w, s0)


def _route(logits):
    lane = lax.broadcasted_iota(jnp.int32, logits.shape, 1)
    big = jnp.int32(V7X_LANES)
    neg = jnp.float32(-jnp.inf)

    def top(mask):
        val = jnp.max(jnp.where(mask, logits, neg), axis=-1, keepdims=True)
        idx = jnp.min(jnp.where(mask & (logits == val), lane, big), axis=-1, keepdims=True)
        return val, idx

    gmask = lane < N_GROUPS
    gmax, gidx = top(gmask)
    p_top = 1.0 / jnp.sum(jnp.where(gmask, jnp.exp(logits - gmax), 0.0), axis=-1, keepdims=True)
    lo = N_GROUPS + gidx * EXPERTS_PER_GROUP
    emask = (lane >= lo) & (lane < lo + EXPERTS_PER_GROUP)
    t1, i1 = top(emask)
    t2, i2 = top(emask & (lane != i1))
    e2 = jnp.exp(t2 - t1)
    g1 = p_top / (1.0 + e2)
    g2 = p_top * e2 / (1.0 + e2)
    return i1 - N_GROUPS, i2 - N_GROUPS, g1, g2


def _mix_moe_body(x_ref, oda_ref, ohg_ref, wo_ref, n2_ref, wr_ref, br_ref,
                  w1_ref, w3_ref, w2_ref, fw_ref, y_ref):
    x1 = (x_ref[...] + _dot(oda_ref[...], wo_ref[0:DA_W, :])
          + _dot(ohg_ref[...], wo_ref[DA_W:DA_W + HG_W, :]))
    h2 = _rms(x1, n2_ref[...])
    hb = h2.astype(BF16)
    hl = (h2 - hb.astype(F32)).astype(BF16)
    logits = (_dot(hb, wr_ref[0]) + _dot(hl, wr_ref[0]) + _dot(hb, wr_ref[1])) + br_ref[...]
    id1, id2, g1, g2 = _route(logits)

    def expert(e, y):
        he = _dot(hb, w1_ref[e])
        he = he * jax.nn.sigmoid(he) * _dot(hb, w3_ref[e])
        c = jnp.where(id1 == e, g1, 0.0) + jnp.where(id2 == e, g2, 0.0)
        return y + c * _dot(he.astype(BF16), w2_ref[e])

    y = lax.fori_loop(0, N_EXPERTS, expert, jnp.zeros_like(x1))
    y_ref[...] = _rms(x1 + y, fw_ref[...])


def _mix_moe(x, oda, ohg, wo_b, norm2_w, wr, br, w1_b, w3_b, w2_b, final_w, tm):
    t = x.shape[0]
    row = lambda i: (i, 0)
    c2 = lambda i: (0, 0)
    c3 = lambda i: (0, 0, 0)
    once = pl.Buffered(1)
    return pl.pallas_call(
        _mix_moe_body,
        grid=(t // tm,),
        in_specs=[pl.BlockSpec((tm, D_MODEL), row),
                  pl.BlockSpec((tm, DA_W), row),
                  pl.BlockSpec((tm, HG_W), row),
                  pl.BlockSpec((D_MODEL, D_MODEL), c2, pipeline_mode=once),
                  pl.BlockSpec((1, D_MODEL), c2),
                  pl.BlockSpec((2, D_MODEL, V7X_LANES), c3, pipeline_mode=once),
                  pl.BlockSpec((1, V7X_LANES), c2),
                  pl.BlockSpec((N_EXPERTS, D_MODEL, D_EXPERT), c3, pipeline_mode=once),
                  pl.BlockSpec((N_EXPERTS, D_MODEL, D_EXPERT), c3, pipeline_mode=once),
                  pl.BlockSpec((N_EXPERTS, D_EXPERT, D_MODEL), c3, pipeline_mode=once),
                  pl.BlockSpec((1, D_MODEL), c2)],
        out_specs=pl.BlockSpec((tm, D_MODEL), row),
        out_shape=jax.ShapeDtypeStruct((t, D_MODEL), F32),
        compiler_params=_cparams("arbitrary"),
        name="mix_moe",
    )(x, oda, ohg, wo_b, norm2_w, wr, br, w1_b, w3_b, w2_b, final_w)


def _prompt_distances():
    tb = ATT_BLOCK
    r = jnp.arange(tb, dtype=jnp.int32)[:, None]
    c = jnp.arange(tb, dtype=jnp.int32)[None, :]
    return jnp.concatenate([r - c, tb + r - c], axis=0)


def _sample_distances(dec_seq):
    t = (jnp.arange(8, dtype=jnp.int32) % 4)[:, None]
    c = jnp.arange(PAGE_SIZE, dtype=jnp.int32)[None, :]
    far = jnp.full((8, PAGE_SIZE), 1 << 20, jnp.int32)
    near = PAGE_SIZE + t - c
    new = jnp.where(c < dec_seq, t - c, -1)
    return jnp.concatenate([far, near, new], axis=0)


def kernel(x_prompt, x_sample, cache_k, cache_v, state_hgrn, page_table, norm1_w, w_in, rel_bias,
           lambda_q1, lambda_k1, lambda_q2, lambda_k2, da_subln_w, hg_lb_logits, hg_norm_w, w_out,
           norm2_w, router_g_w, router_g_b, router_e_w, router_e_b, expert_w1, expert_w3,
           expert_w2, final_norm_w):
    batch, seq, _ = x_prompt.shape
    nb, dec_seq, _ = x_sample.shape
    assert dec_seq == 4 and seq % ATT_BLOCK == 0 and seq % HG_CHUNK == 0
    l = 0

    lam = (jnp.exp(jnp.sum(lambda_q1[l].astype(F32) * lambda_k1[l].astype(F32)))
           - jnp.exp(jnp.sum(lambda_q2[l].astype(F32) * lambda_k2[l].astype(F32))) + LAM_INIT)
    lam = lam.reshape(1)
    lb = jnp.cumsum(jax.nn.softmax(hg_lb_logits.astype(F32), axis=0), axis=0)[l].reshape(1, HG_W)
    w_in_b = w_in[l].astype(BF16)
    wo_b = w_out[l].astype(BF16)
    w1_b = expert_w1[l].astype(BF16)
    w3_b = expert_w3[l].astype(BF16)
    w2_b = expert_w2[l].astype(BF16)
    pad = V7X_LANES - N_GROUPS - N_EXPERTS
    wr_f = jnp.pad(jnp.concatenate([router_g_w[l], router_e_w[l]], axis=1), ((0, 0), (0, pad)))
    wr_hi = wr_f.astype(BF16)
    wr = jnp.stack([wr_hi, (wr_f - wr_hi.astype(F32)).astype(BF16)])
    br = jnp.pad(jnp.concatenate([router_g_b[l], router_e_b[l]]), (0, pad)).reshape(1, V7X_LANES)
    n1 = norm1_w[l].reshape(1, D_MODEL)
    n2 = norm2_w[l].reshape(1, D_MODEL)
    fw = final_norm_w.reshape(1, D_MODEL)
    subln = da_subln_w[l].reshape(1, DA_V)
    hgn = hg_norm_w[l].reshape(1, HG_V)

    tp = batch * seq
    qb, k_p, v_p, kb, vb, hg = _proj(x_prompt.reshape(tp, D_MODEL), n1, w_in_b, 512)
    bias_p = _bias_tiles(rel_bias, _prompt_distances())
    oda = _attn_prompt(lam, qb, kb, vb, bias_p, subln, batch, seq)
    s0p = jnp.zeros((batch, HG_HEADS, HG_K, HG_V), F32)
    ohg, s_p = _hgrn(hg, lb, hgn, s0p, batch, seq, HG_CHUNK, HG_DIAG, HG_CHUNK)
    y_p = _mix_moe(x_prompt.reshape(tp, D_MODEL), oda, ohg, wo_b, n2, wr, br,
                   w1_b, w3_b, w2_b, fw, 512)

    ts = nb * dec_seq
    qs, k_s, v_s, ksb, vsb, hgs = _proj(x_sample.reshape(ts, D_MODEL), n1, w_in_b, ts)
    bias_s = _bias_tiles(rel_bias, _sample_distances(dec_seq))
    bias_s = bias_s.reshape(DA_HEADS, 3, 8, PAGE_SIZE).transpose(1, 0, 2, 3).reshape(
        3, DA_HEADS * 8, PAGE_SIZE)
    q_rep = jnp.tile(qs.reshape(nb, dec_seq, DA_W), (1, 2 * DA_HEADS, 1))
    pad8 = lambda a: jnp.pad(a.reshape(nb, dec_seq, DA_W), ((0, 0), (0, 8 - dec_seq), (0, 0)))
    n_pool = cache_k.shape[1]
    oda_s = _attn_sample(page_table, lam, q_rep, bias_s, pad8(ksb), pad8(vsb), subln,
                         cache_k[l].reshape(n_pool, PAGE_SIZE, DA_W),
                         cache_v[l].reshape(n_pool, PAGE_SIZE, DA_W))
    hgs_pad = jnp.pad(hgs.reshape(nb, dec_seq, 4 * HG_W),
                      ((0, 0), (0, SAMPLE_CHUNK - dec_seq), (0, 0)))
    ohg_s, s_s = _hgrn(hgs_pad.reshape(nb * SAMPLE_CHUNK, 4 * HG_W), lb, hgn, state_hgrn[l],
                       nb, SAMPLE_CHUNK, SAMPLE_CHUNK, SAMPLE_CHUNK, dec_seq)
    ohg_s = ohg_s.reshape(nb, SAMPLE_CHUNK, HG_W)[:, :dec_seq].reshape(ts, HG_W)
    y_s = _mix_moe(x_sample.reshape(ts, D_MODEL), oda_s.reshape(ts, DA_W), ohg_s, wo_b, n2, wr, br,
                   w1_b, w3_b, w2_b, fw, ts)

    return (y_p.reshape(batch, seq, D_MODEL),
            y_s.reshape(nb, dec_seq, D_MODEL),
            k_p.reshape(1, batch, seq, DA_HEADS, 2 * DA_QK),
            v_p.reshape(1, batch, seq, DA_HEADS, DA_V),
            s_p.reshape(1, batch, HG_HEADS, HG_K, HG_V),
            k_s.reshape(1, nb, dec_seq, DA_HEADS, 2 * DA_QK),
            v_s.reshape(1, nb, dec_seq, DA_HEADS, DA_V),
            s_s.reshape(1, nb, HG_HEADS, HG_K, HG_V))
```

```python
import functools
import math

import jax
import jax.numpy as jnp
from jax import lax
from jax.experimental import pallas as pl
from jax.experimental.pallas import tpu as pltpu

F32 = jnp.float32
BF16 = jnp.bfloat16

D_MODEL = 1024
PAGE_SIZE = 128
DA_HEADS = 4
DA_QK = 64
DA_V = 128
DA_W = DA_HEADS * DA_V
HG_HEADS = 4
HG_K = 128
HG_V = 128
HG_W = HG_HEADS * HG_V
N_BUCKETS = 32
MAX_DISTANCE = 128
N_GROUPS = 4
EXPERTS_PER_GROUP = 4
N_EXPERTS = N_GROUPS * EXPERTS_PER_GROUP
D_EXPERT = 256
EPS = 1e-6
IN_TOTAL = 7 * 512
LAM_INIT = 0.8 - 0.6 * math.exp(-0.3 * 0)

NEG_BIG = -1e30
V7X_LANES = 128
VMEM_LIMIT = 56 * 1024 * 1024

ATT_BLOCK = 512
HG_CHUNK = 128
HG_DIAG = 16
SAMPLE_CHUNK = 16
PAGES_PER_STEP = 8


def _cparams(*sem):
    return pltpu.CompilerParams(dimension_semantics=sem, vmem_limit_bytes=VMEM_LIMIT)


def _dot(a, b):
    return jnp.dot(a, b, preferred_element_type=F32)


def _dot_nt(a, b):
    return lax.dot_general(a, b, (((1,), (1,)), ((), ())), preferred_element_type=F32)


def _dot_tn(a, b):
    return lax.dot_general(a, b, (((0,), (0,)), ((), ())), preferred_element_type=F32)


def _rms(x, w):
    return x * lax.rsqrt(jnp.mean(x * x, axis=-1, keepdims=True) + EPS) * w


def _bias_body(rb_ref, n_ref, o_ref, *, interleaved):
    h = pl.program_id(0)
    n = n_ref[...]
    if interleaved:
        lane = lax.broadcasted_iota(jnp.int32, n.shape, 1)
        n = jnp.where(lane % DA_HEADS == h, n, -1)
    nn = jnp.maximum(n, 0)
    max_exact = N_BUCKETS // 2
    nf = jnp.maximum(nn, max_exact).astype(F32)
    large = max_exact + (jnp.log(nf / max_exact) / math.log(MAX_DISTANCE / max_exact)
                         * (N_BUCKETS - max_exact)).astype(jnp.int32)
    large = jnp.minimum(large, N_BUCKETS - 1)
    bucket = jnp.where(nn < max_exact, nn, large)
    bias = jnp.zeros(n.shape, F32)
    for b in range(N_BUCKETS):
        bias = jnp.where(bucket == b, rb_ref[b, h], bias)
    o_ref[0] = jnp.where(n >= 0, bias, NEG_BIG)


def _bias_tiles(rel_bias, n, interleaved):
    r, c = n.shape
    return pl.pallas_call(
        functools.partial(_bias_body, interleaved=interleaved),
        grid=(DA_HEADS,),
        in_specs=[pl.BlockSpec(memory_space=pltpu.SMEM),
                  pl.BlockSpec((r, c), lambda h: (0, 0))],
        out_specs=pl.BlockSpec((1, r, c), lambda h: (h, 0, 0)),
        out_shape=jax.ShapeDtypeStruct((DA_HEADS, r, c), F32),
        compiler_params=_cparams("arbitrary"),
        name="bias_tiles",
    )(rel_bias, n)


def _proj_body(x_ref, nw_ref, w_ref, q_ref, k_ref, v_ref, kb_ref, vb_ref, hg_ref):
    hb = _rms(x_ref[...], nw_ref[...]).astype(BF16)
    q_ref[...] = (_dot(hb, w_ref[:, 0:512]) * (DA_QK ** -0.5)).astype(BF16)
    tm = x_ref.shape[0]
    k = _dot(hb, w_ref[:, 512:1024])
    kb_ref[...] = k.astype(BF16)
    v = _dot(hb, w_ref[:, 1024:1536])
    vb_ref[...] = v.astype(BF16)
    for h in range(DA_HEADS):
        k_ref[pl.ds(h, tm, stride=DA_HEADS), :] = k[:, h * DA_V:(h + 1) * DA_V]
        v_ref[pl.ds(h, tm, stride=DA_HEADS), :] = v[:, h * DA_V:(h + 1) * DA_V]
    hg_ref[...] = _dot(hb, w_ref[:, 1536:IN_TOTAL])


def _proj(x, norm_w, w_in_b, tm):
    t = x.shape[0]
    row = lambda i: (i, 0)
    const = lambda i: (0, 0)
    return pl.pallas_call(
        _proj_body,
        grid=(t // tm,),
        in_specs=[pl.BlockSpec((tm, D_MODEL), row),
                  pl.BlockSpec((1, D_MODEL), const),
                  pl.BlockSpec((D_MODEL, IN_TOTAL), const)],
        out_specs=[pl.BlockSpec((tm, 512), row),
                   pl.BlockSpec((tm * DA_HEADS, DA_V), row),
                   pl.BlockSpec((tm * DA_HEADS, DA_V), row),
                   pl.BlockSpec((tm, 512), row),
                   pl.BlockSpec((tm, 512), row),
                   pl.BlockSpec((tm, 2048), row)],
        out_shape=[jax.ShapeDtypeStruct((t, 512), BF16),
                   jax.ShapeDtypeStruct((t * DA_HEADS, DA_V), F32),
                   jax.ShapeDtypeStruct((t * DA_HEADS, DA_V), F32),
                   jax.ShapeDtypeStruct((t, 512), BF16),
                   jax.ShapeDtypeStruct((t, 512), BF16),
                   jax.ShapeDtypeStruct((t, 2048), F32)],
        compiler_params=_cparams("arbitrary"),
        name="proj",
    )(x, norm_w, w_in_b)


def _softmax_update(s, v, m_ref, l_ref, acc_ref, idx):
    m_prev = m_ref[idx]
    m_new = jnp.maximum(m_prev, jnp.max(s, axis=-1, keepdims=True))
    alpha = jnp.exp(m_prev - m_new)
    p = jnp.exp(s - m_new)
    l_ref[idx] = alpha * l_ref[idx] + jnp.sum(p, axis=-1, keepdims=True)
    acc_ref[idx] = alpha * acc_ref[idx] + _dot(p.astype(BF16), v)
    m_ref[idx] = m_new


def _sub_ln(o, w):
    return _rms(o, w) * (1.0 - LAM_INIT)


def _attn_prompt_body(lam_ref, q_ref, k_ref, v_ref, bias_ref, w_ref, o_ref,
                      vt_ref, m_ref, l_ref, acc_ref):
    qi = pl.program_id(2)
    tb = ATT_BLOCK
    nblk = vt_ref.shape[0]

    @pl.when(qi == 0)
    def _():
        for j in range(nblk):
            vt_ref[j] = v_ref[j * tb:(j + 1) * tb, :].T

    q = q_ref[...]
    lane = lax.broadcasted_iota(jnp.int32, q.shape, 1)
    zero = jnp.zeros_like(q)
    qm = (jnp.where(lane < DA_QK, q, zero), jnp.where(lane >= DA_QK, q, zero))

    m_ref[...] = jnp.full(m_ref.shape, NEG_BIG, F32)
    l_ref[...] = jnp.zeros(l_ref.shape, F32)
    acc_ref[...] = jnp.zeros(acc_ref.shape, F32)

    def block(kj, carry):
        start = pl.multiple_of(kj * tb, tb)
        kb = k_ref[pl.ds(start, tb), :]
        vt = vt_ref[kj]
        bias = bias_ref[0, jnp.maximum(kj - qi + 2, 0)]
        for mp in range(2):
            s = _dot_nt(kb, qm[mp]) + bias
            m_prev = m_ref[mp]
            m_new = jnp.maximum(m_prev, jnp.max(s, axis=0, keepdims=True))
            alpha = jnp.exp(m_prev - m_new)
            p = jnp.exp(s - m_new)
            l_ref[mp] = alpha * l_ref[mp] + jnp.sum(p, axis=0, keepdims=True)
            acc_ref[mp] = alpha * acc_ref[mp] + _dot(vt, p.astype(BF16))
            m_ref[mp] = m_new
        return carry

    lax.fori_loop(0, qi + 1, block, 0)

    o = acc_ref[0] / l_ref[0] - lam_ref[0] * (acc_ref[1] / l_ref[1])
    o = o * lax.rsqrt(jnp.mean(o * o, axis=0, keepdims=True) + EPS) * w_ref[...]
    o_ref[...] = (o * (1.0 - LAM_INIT)).T.astype(BF16)


def _attn_prompt(lam, qb, kb, vb, bias, subln_col, batch, seq):
    tb = ATT_BLOCK
    nq = seq // tb
    return pl.pallas_call(
        _attn_prompt_body,
        grid=(batch, DA_HEADS, nq),
        in_specs=[pl.BlockSpec(memory_space=pltpu.SMEM),
                  pl.BlockSpec((tb, DA_V), lambda b, h, i: (b * nq + i, h)),
                  pl.BlockSpec((seq, DA_V), lambda b, h, i: (b, h)),
                  pl.BlockSpec((seq, DA_V), lambda b, h, i: (b, h)),
                  pl.BlockSpec((1, 3, tb, tb), lambda b, h, i: (h, 0, 0, 0)),
                  pl.BlockSpec((DA_V, 1), lambda b, h, i: (0, 0))],
        out_specs=pl.BlockSpec((tb, DA_V), lambda b, h, i: (b * nq + i, h)),
        out_shape=jax.ShapeDtypeStruct((batch * seq, DA_W), BF16),
        scratch_shapes=[pltpu.VMEM((nq, DA_V, tb), BF16),
                        pltpu.VMEM((2, 1, tb), F32),
                        pltpu.VMEM((2, 1, tb), F32),
                        pltpu.VMEM((2, DA_V, tb), F32)],
        compiler_params=_cparams("arbitrary", "arbitrary", "arbitrary"),
        name="attn_prompt",
    )(lam, qb, kb, vb, bias, subln_col)


def _attn_sample_body(pt_ref, lam_ref, q_ref, bias_ref, kn_ref, vn_ref, w_ref, *rest):
    npg = PAGES_PER_STEP
    k_refs = rest[:npg]
    v_refs = rest[npg:2 * npg]
    o_ref = rest[2 * npg]
    m_ref, l_ref, acc_ref = rest[2 * npg + 1:]
    j = pl.program_id(1)
    last = pl.num_programs(1) - 1
    rows = DA_HEADS * 8

    @pl.when(j == 0)
    def _():
        m_ref[...] = jnp.full(m_ref.shape, NEG_BIG, F32)
        l_ref[...] = jnp.zeros(l_ref.shape, F32)
        acc_ref[...] = jnp.zeros(acc_ref.shape, F32)

    q = q_ref[0]
    lane = lax.broadcasted_iota(jnp.int32, q.shape, 1)
    row = lax.broadcasted_iota(jnp.int32, q.shape, 0)
    w = jnp.where(lane // DA_QK == (row // 4) % 2, q, jnp.zeros_like(q))

    far = bias_ref[0]
    near = bias_ref[1]
    s_parts = []
    for i in range(npg):
        kp = k_refs[i][0].astype(BF16)
        bias = far if i < npg - 1 else jnp.where(j == last, near, far)
        s_parts.append(_dot_nt(w, kp) + bias)
    s = jnp.concatenate(s_parts, axis=1)
    vcat = jnp.concatenate([v_refs[i][0].astype(BF16) for i in range(npg)], axis=0)
    _softmax_update(s, vcat, m_ref, l_ref, acc_ref, 0)

    @pl.when(j == last)
    def _():
        s_new = _dot_nt(w, kn_ref[0]) + bias_ref[2][:, 0:rows]
        _softmax_update(s_new, vn_ref[0], m_ref, l_ref, acc_ref, 0)
        normed = acc_ref[0] / l_ref[0]
        for h in range(DA_HEADS):
            blk = normed[h * 8:(h + 1) * 8, :]
            o = blk[0:4] - lam_ref[0] * blk[4:8]
            o_ref[0, :, h * DA_V:(h + 1) * DA_V] = _sub_ln(o, w_ref[...]).astype(BF16)


def _attn_sample(page_table, lam, q_rows, bias, k_new, v_new, subln_w, cache_k, cache_v):
    nb, n_pages = page_table.shape
    npg = PAGES_PER_STEP
    rows = DA_HEADS * 8
    page_rows = PAGE_SIZE * DA_HEADS

    def page_spec(i):
        return pl.BlockSpec((1, page_rows, DA_V),
                            lambda b, j, pt: (pt[b, j * npg + i], 0, 0))

    seq_spec = lambda shape: pl.BlockSpec(shape, lambda b, j, pt: (b, 0, 0))
    grid_spec = pltpu.PrefetchScalarGridSpec(
        num_scalar_prefetch=1,
        grid=(nb, n_pages // npg),
        in_specs=[pl.BlockSpec(memory_space=pltpu.SMEM),
                  seq_spec((1, rows, DA_V)),
                  pl.BlockSpec((3, rows, page_rows), lambda b, j, pt: (0, 0, 0)),
                  seq_spec((1, rows, DA_V)),
                  seq_spec((1, rows, DA_V)),
                  pl.BlockSpec((1, DA_V), lambda b, j, pt: (0, 0))]
                 + [page_spec(i) for i in range(npg)] * 2,
        out_specs=seq_spec((1, 4, DA_W)),
        scratch_shapes=[pltpu.VMEM((1, rows, 1), F32),
                        pltpu.VMEM((1, rows, 1), F32),
                        pltpu.VMEM((1, rows, DA_V), F32)],
    )
    return pl.pallas_call(
        _attn_sample_body,
        grid_spec=grid_spec,
        out_shape=jax.ShapeDtypeStruct((nb, 4, DA_W), BF16),
        compiler_params=_cparams("arbitrary", "arbitrary"),
        name="attn_sample",
    )(page_table, lam, q_rows, bias, k_new, v_new, subln_w,
      *([cache_k] * npg), *([cache_v] * npg))


def _split3(x):
    hi = x.astype(BF16)
    r = x - hi.astype(F32)
    mid = r.astype(BF16)
    lo = (r - mid.astype(F32)).astype(BF16)
    return jnp.concatenate([hi, mid, lo], axis=1)


def _merge3(p):
    w = p.shape[1] // 3
    return p[:, 0:w] + p[:, w:2 * w] + p[:, 2 * w:3 * w]


def _hgrn_head(hq, hf, hi, hgate, lb, norm_w, st, chunk, diag, valid):
    c = chunk
    row = lax.broadcasted_iota(jnp.int32, (c, HG_K), 0)
    q = hq * jax.nn.sigmoid(hq) * (HG_K ** -0.5)
    f = lb + (1.0 - lb) * jax.nn.sigmoid(hf)
    if valid < c:
        f = jnp.where(row < valid, f, 1.0)
        q = jnp.where(row < valid, q, 0.0)
    g = jnp.log(f)
    kk = 1.0 - f
    vb = hi.astype(BF16)

    r2 = lax.broadcasted_iota(jnp.int32, (c, c), 0)
    c2 = lax.broadcasted_iota(jnp.int32, (c, c), 1)
    tri = jnp.where(c2 <= r2, 1.0, 0.0).astype(BF16)
    gc = _merge3(_dot(tri, _split3(g)))

    a = jnp.zeros((c, c), F32)
    rmod = row % diag
    for d in range(diag):
        if d == 0:
            x = q * kk
        else:
            e = jnp.where(rmod >= d, gc - pltpu.roll(gc, d, axis=0), NEG_BIG)
            x = q * pltpu.roll(kk, d, axis=0) * jnp.exp(e)
        a = a + jnp.where(r2 - c2 == d, jnp.sum(x, axis=-1, keepdims=True), 0.0)

    levels = []
    half = diag
    while half < c:
        levels.append(half)
        half *= 2
    if levels:
        gsplit = _split3(gc)
        for half in levels:
            sel = jnp.where(c2 == (r2 // (2 * half)) * (2 * half) + half - 1, 1.0, 0.0).astype(BF16)
            bnd = _merge3(_dot(sel, gsplit))
            second = (row % (2 * half)) >= half
            qh = q * jnp.exp(jnp.where(second, gc - bnd, NEG_BIG))
            kh = kk * jnp.exp(jnp.where(second, NEG_BIG, bnd - gc))
            al = _dot_nt(qh.astype(BF16), kh.astype(BF16))
            a = a + jnp.where(r2 // (2 * half) == c2 // (2 * half), al, 0.0)

    g_last = gc[c - 1:c, :]
    o = _dot_nt((q * jnp.exp(gc)).astype(BF16), st.astype(BF16)) + _dot(a.astype(BF16), vb)
    kl = (kk * jnp.exp(g_last - gc)).astype(BF16)
    st_new = st * jnp.exp(g_last) + _dot_tn(vb, kl)
    o = _rms(o, norm_w) * (hgate * jax.nn.sigmoid(hgate))
    return o, st_new


def _hgrn_body(hg_ref, lb_ref, nw_ref, s0_ref, o_ref, sf_ref, st_ref, *, chunk, diag, valid):
    ci = pl.program_id(1)

    @pl.when(ci == 0)
    def _():
        for h in range(HG_HEADS):
            st_ref[h] = s0_ref[0, h].T

    for h in range(HG_HEADS):
        sl = lambda part: hg_ref[:, part * HG_W + h * HG_K: part * HG_W + (h + 1) * HG_K]
        o, st_new = _hgrn_head(sl(0), sl(1), sl(2), sl(3),
                               lb_ref[:, h * HG_K:(h + 1) * HG_K], nw_ref[...],
                               st_ref[h], chunk, diag, valid)
        st_ref[h] = st_new
        o_ref[:, h * HG_V:(h + 1) * HG_V] = o.astype(BF16)

    @pl.when(ci == pl.num_programs(1) - 1)
    def _():
        for h in range(HG_HEADS):
            sf_ref[0, h] = st_ref[h].T


def _hgrn(hg, lb, norm_w, s0, batch, length, chunk, diag, valid):
    nc = length // chunk
    body = functools.partial(_hgrn_body, chunk=chunk, diag=diag, valid=valid)
    state_spec = pl.BlockSpec((1, HG_HEADS, HG_K, HG_V), lambda b, c: (b, 0, 0, 0))
    return pl.pallas_call(
        body,
        grid=(batch, nc),
        in_specs=[pl.BlockSpec((chunk, 4 * HG_W), lambda b, c: (b * nc + c, 0)),
                  pl.BlockSpec((1, HG_W), lambda b, c: (0, 0)),
                  pl.BlockSpec((1, HG_V), lambda b, c: (0, 0)),
                  state_spec],
        out_specs=[pl.BlockSpec((chunk, HG_W), lambda b, c: (b * nc + c, 0)), state_spec],
        out_shape=[jax.ShapeDtypeStruct((batch * length, HG_W), BF16),
                   jax.ShapeDtypeStruct((batch, HG_HEADS, HG_K, HG_V), F32)],
        scratch_shapes=[pltpu.VMEM((HG_HEADS, HG_V, HG_K), F32)],
        compiler_params=_cparams("arbitrary", "arbitrary"),
        name="hgrn",
    )(hg, lb, norm_w, s0)


def _route(logits):
    lane = lax.broadcasted_iota(jnp.int32, logits.shape, 1)
    big = jnp.int32(V7X_LANES)
    neg = jnp.float32(-jnp.inf)

    def top(mask):
        val = jnp.max(jnp.where(mask, logits, neg), axis=-1, keepdims=True)
        idx = jnp.min(jnp.where(mask & (logits == val), lane, big), axis=-1, keepdims=True)
        return val, idx

    gmask = lane < N_GROUPS
    gmax, gidx = top(gmask)
    p_top = 1.0 / jnp.sum(jnp.where(gmask, jnp.exp(logits - gmax), 0.0), axis=-1, keepdims=True)
    lo = N_GROUPS + gidx * EXPERTS_PER_GROUP
    emask = (lane >= lo) & (lane < lo + EXPERTS_PER_GROUP)
    t1, i1 = top(emask)
    t2, i2 = top(emask & (lane != i1))
    e2 = jnp.exp(t2 - t1)
    g1 = p_top / (1.0 + e2)
    g2 = p_top * e2 / (1.0 + e2)
    return i1 - N_GROUPS, i2 - N_GROUPS, g1, g2


def _mix_moe_body(x_ref, oda_ref, ohg_ref, wo_ref, n2_ref, wr_ref, br_ref,
                  w1_ref, w3_ref, w2_ref, fw_ref, y_ref):
    x1 = (x_ref[...] + _dot(oda_ref[...], wo_ref[0:DA_W, :])
          + _dot(ohg_ref[...], wo_ref[DA_W:DA_W + HG_W, :]))
    h2 = _rms(x1, n2_ref[...])
    hb = h2.astype(BF16)
    hl = (h2 - hb.astype(F32)).astype(BF16)
    logits = (_dot(hb, wr_ref[0]) + _dot(hl, wr_ref[0]) + _dot(hb, wr_ref[1])) + br_ref[...]
    id1, id2, g1, g2 = _route(logits)

    def expert(e, y):
        he = _dot(hb, w1_ref[e])
        he = he * jax.nn.sigmoid(he) * _dot(hb, w3_ref[e])
        c = jnp.where(id1 == e, g1, 0.0) + jnp.where(id2 == e, g2, 0.0)
        return y + c * _dot(he.astype(BF16), w2_ref[e])

    y = lax.fori_loop(0, N_EXPERTS, expert, jnp.zeros_like(x1))
    y_ref[...] = _rms(x1 + y, fw_ref[...])


def _mix_moe(x, oda, ohg, wo_b, norm2_w, wr, br, w1_b, w3_b, w2_b, final_w, tm):
    t = x.shape[0]
    row = lambda i: (i, 0)
    c2 = lambda i: (0, 0)
    c3 = lambda i: (0, 0, 0)
    once = pl.Buffered(1)
    return pl.pallas_call(
        _mix_moe_body,
        grid=(t // tm,),
        in_specs=[pl.BlockSpec((tm, D_MODEL), row),
                  pl.BlockSpec((tm, DA_W), row),
                  pl.BlockSpec((tm, HG_W), row),
                  pl.BlockSpec((D_MODEL, D_MODEL), c2, pipeline_mode=once),
                  pl.BlockSpec((1, D_MODEL), c2),
                  pl.BlockSpec((2, D_MODEL, V7X_LANES), c3, pipeline_mode=once),
                  pl.BlockSpec((1, V7X_LANES), c2),
                  pl.BlockSpec((N_EXPERTS, D_MODEL, D_EXPERT), c3, pipeline_mode=once),
                  pl.BlockSpec((N_EXPERTS, D_MODEL, D_EXPERT), c3, pipeline_mode=once),
                  pl.BlockSpec((N_EXPERTS, D_EXPERT, D_MODEL), c3, pipeline_mode=once),
                  pl.BlockSpec((1, D_MODEL), c2)],
        out_specs=pl.BlockSpec((tm, D_MODEL), row),
        out_shape=jax.ShapeDtypeStruct((t, D_MODEL), F32),
        compiler_params=_cparams("arbitrary"),
        name="mix_moe",
    )(x, oda, ohg, wo_b, norm2_w, wr, br, w1_b, w3_b, w2_b, final_w)


def _prompt_distances():
    tb = ATT_BLOCK
    key = jnp.arange(tb, dtype=jnp.int32)[:, None]
    qry = jnp.arange(tb, dtype=jnp.int32)[None, :]
    return jnp.concatenate([2 * tb + qry - key, tb + qry - key, qry - key], axis=0)


def _sample_distances(dec_seq):
    t = (jnp.arange(8, dtype=jnp.int32) % 4)[:, None]
    key = (jnp.arange(PAGE_SIZE * DA_HEADS, dtype=jnp.int32) // DA_HEADS)[None, :]
    far = jnp.full((8, PAGE_SIZE * DA_HEADS), 1 << 20, jnp.int32)
    near = PAGE_SIZE + t - key
    new = jnp.where(key < dec_seq, t - key, -1)
    return jnp.concatenate([far, near, new], axis=0)


def kernel(x_prompt, x_sample, cache_k, cache_v, state_hgrn, page_table, norm1_w, w_in, rel_bias,
           lambda_q1, lambda_k1, lambda_q2, lambda_k2, da_subln_w, hg_lb_logits, hg_norm_w, w_out,
           norm2_w, router_g_w, router_g_b, router_e_w, router_e_b, expert_w1, expert_w3,
           expert_w2, final_norm_w):
    batch, seq, _ = x_prompt.shape
    nb, dec_seq, _ = x_sample.shape
    assert dec_seq == 4 and seq % ATT_BLOCK == 0 and seq % HG_CHUNK == 0
    l = 0

    lam = (jnp.exp(jnp.sum(lambda_q1[l].astype(F32) * lambda_k1[l].astype(F32)))
           - jnp.exp(jnp.sum(lambda_q2[l].astype(F32) * lambda_k2[l].astype(F32))) + LAM_INIT)
    lam = lam.reshape(1)
    lb = jnp.cumsum(jax.nn.softmax(hg_lb_logits.astype(F32), axis=0), axis=0)[l].reshape(1, HG_W)
    w_in_b = w_in[l].astype(BF16)
    wo_b = w_out[l].astype(BF16)
    w1_b = expert_w1[l].astype(BF16)
    w3_b = expert_w3[l].astype(BF16)
    w2_b = expert_w2[l].astype(BF16)
    pad = V7X_LANES - N_GROUPS - N_EXPERTS
    wr_f = jnp.pad(jnp.concatenate([router_g_w[l], router_e_w[l]], axis=1), ((0, 0), (0, pad)))
    wr_hi = wr_f.astype(BF16)
    wr = jnp.stack([wr_hi, (wr_f - wr_hi.astype(F32)).astype(BF16)])
    br = jnp.pad(jnp.concatenate([router_g_b[l], router_e_b[l]]), (0, pad)).reshape(1, V7X_LANES)
    n1 = norm1_w[l].reshape(1, D_MODEL)
    n2 = norm2_w[l].reshape(1, D_MODEL)
    fw = final_norm_w.reshape(1, D_MODEL)
    subln = da_subln_w[l].reshape(1, DA_V)
    hgn = hg_norm_w[l].reshape(1, HG_V)

    tp = batch * seq
    qb, k_p, v_p, kb, vb, hg = _proj(x_prompt.reshape(tp, D_MODEL), n1, w_in_b, 512)
    bias_p = _bias_tiles(rel_bias, _prompt_distances(), False).reshape(
        DA_HEADS, 3, ATT_BLOCK, ATT_BLOCK)
    oda = _attn_prompt(lam, qb, kb, vb, bias_p, subln.reshape(DA_V, 1), batch, seq)
    s0p = jnp.zeros((batch, HG_HEADS, HG_K, HG_V), F32)
    ohg, s_p = _hgrn(hg, lb, hgn, s0p, batch, seq, HG_CHUNK, HG_DIAG, HG_CHUNK)
    y_p = _mix_moe(x_prompt.reshape(tp, D_MODEL), oda, ohg, wo_b, n2, wr, br,
                   w1_b, w3_b, w2_b, fw, 512)

    ts = nb * dec_seq
    qs, k_s, v_s, ksb, vsb, hgs = _proj(x_sample.reshape(ts, D_MODEL), n1, w_in_b, ts)
    page_rows = PAGE_SIZE * DA_HEADS
    bias_s = _bias_tiles(rel_bias, _sample_distances(dec_seq), True)
    bias_s = bias_s.reshape(DA_HEADS, 3, 8, page_rows).transpose(1, 0, 2, 3).reshape(
        3, DA_HEADS * 8, page_rows)
    q_rows = jnp.broadcast_to(
        qs.reshape(nb, dec_seq, DA_HEADS, 1, DA_V).transpose(0, 2, 3, 1, 4),
        (nb, DA_HEADS, 2, dec_seq, DA_V)).reshape(nb, DA_HEADS * 8, DA_V)
    new_rows = lambda a: jnp.pad(a.reshape(nb, dec_seq * DA_HEADS, DA_V),
                                 ((0, 0), (0, (8 - dec_seq) * DA_HEADS), (0, 0)))
    n_pool = cache_k.shape[1]
    oda_s = _attn_sample(page_table, lam, q_rows, bias_s, new_rows(ksb), new_rows(vsb), subln,
                         cache_k[l].reshape(n_pool, page_rows, DA_V),
                         cache_v[l].reshape(n_pool, page_rows, DA_V))
    hgs_pad = jnp.pad(hgs.reshape(nb, dec_seq, 4 * HG_W),
                      ((0, 0), (0, SAMPLE_CHUNK - dec_seq), (0, 0)))
    ohg_s, s_s = _hgrn(hgs_pad.reshape(nb * SAMPLE_CHUNK, 4 * HG_W), lb, hgn, state_hgrn[l],
                       nb, SAMPLE_CHUNK, SAMPLE_CHUNK, SAMPLE_CHUNK, dec_seq)
    ohg_s = ohg_s.reshape(nb, SAMPLE_CHUNK, HG_W)[:, :dec_seq].reshape(ts, HG_W)
    y_s = _mix_moe(x_sample.reshape(ts, D_MODEL), oda_s.reshape(ts, DA_W), ohg_s, wo_b, n2, wr, br,
                   w1_b, w3_b, w2_b, fw, ts)

    return (y_p.reshape(batch, seq, D_MODEL),
            y_s.reshape(nb, dec_seq, D_MODEL),
            k_p.reshape(1, batch, seq, DA_HEADS, 2 * DA_QK),
            v_p.reshape(1, batch, seq, DA_HEADS, DA_V),
            s_p.reshape(1, batch, HG_HEADS, HG_K, HG_V),
            k_s.reshape(1, nb, dec_seq, DA_HEADS, 2 * DA_QK),
            v_s.reshape(1, nb, dec_seq, DA_HEADS, DA_V),
            s_s.reshape(1, nb, HG_HEADS, HG_K, HG_V))
```

```python
import functools
import math

import jax
import jax.numpy as jnp
from jax import lax
from jax.experimental import pallas as pl
from jax.experimental.pallas import tpu as pltpu

F32 = jnp.float32
BF16 = jnp.bfloat16

D_MODEL = 1024
PAGE_SIZE = 128
DA_HEADS = 4
DA_QK = 64
DA_V = 128
DA_W = DA_HEADS * DA_V
HG_HEADS = 4
HG_K = 128
HG_V = 128
HG_W = HG_HEADS * HG_V
N_BUCKETS = 32
MAX_DISTANCE = 128
N_GROUPS = 4
EXPERTS_PER_GROUP = 4
N_EXPERTS = N_GROUPS * EXPERTS_PER_GROUP
D_EXPERT = 256
EPS = 1e-6
IN_TOTAL = 7 * 512
LAM_INIT = 0.8 - 0.6 * math.exp(-0.3 * 0)

NEG_BIG = -1e30
V7X_LANES = 128
VMEM_LIMIT = 56 * 1024 * 1024

def _t5_bucket_py(n):
    max_exact = N_BUCKETS // 2
    if n < max_exact:
        return n
    return min(max_exact + int(math.log(n / max_exact) / math.log(MAX_DISTANCE / max_exact)
                               * (N_BUCKETS - max_exact)), N_BUCKETS - 1)


LAST_BUCKET_START = next(n for n in range(MAX_DISTANCE + 1) if _t5_bucket_py(n) == N_BUCKETS - 1)

ATT_BLOCK = 512
SUM_ROWS = 16
HG_CHUNK = 128
HG_SUBCHUNKS = 2
HG_DIAG = 16
SAMPLE_CHUNK = 16
PAGES_PER_STEP = 16


def _cparams(*sem):
    return pltpu.CompilerParams(dimension_semantics=sem, vmem_limit_bytes=VMEM_LIMIT)


def _dot(a, b):
    return jnp.dot(a, b, preferred_element_type=F32)


def _dot_nt(a, b):
    return lax.dot_general(a, b, (((1,), (1,)), ((), ())), preferred_element_type=F32)


def _dot_tn(a, b):
    return lax.dot_general(a, b, (((0,), (0,)), ((), ())), preferred_element_type=F32)


def _rms(x, w):
    return x * lax.rsqrt(jnp.mean(x * x, axis=-1, keepdims=True) + EPS) * w


def _bias_body(rb_ref, n_ref, o_ref, *, interleaved):
    h = pl.program_id(0)
    n = n_ref[...]
    if interleaved:
        lane = lax.broadcasted_iota(jnp.int32, n.shape, 1)
        n = jnp.where(lane % DA_HEADS == h, n, -1)
    nn = jnp.maximum(n, 0)
    max_exact = N_BUCKETS // 2
    nf = jnp.maximum(nn, max_exact).astype(F32)
    large = max_exact + jnp.floor(jnp.log(nf / max_exact) / math.log(MAX_DISTANCE / max_exact)
                                  * (N_BUCKETS - max_exact)).astype(jnp.int32)
    large = jnp.minimum(large, N_BUCKETS - 1)
    bucket = jnp.where(nn < max_exact, nn, large)
    bias = jnp.zeros(n.shape, F32)
    for b in range(N_BUCKETS):
        bias = jnp.where(bucket == b, rb_ref[b, h], bias)
    o_ref[0] = jnp.where(n >= 0, bias, NEG_BIG)


def _bias_tiles(rel_bias, n, interleaved):
    r, c = n.shape
    return pl.pallas_call(
        functools.partial(_bias_body, interleaved=interleaved),
        grid=(DA_HEADS,),
        in_specs=[pl.BlockSpec(memory_space=pltpu.SMEM),
                  pl.BlockSpec((r, c), lambda h: (0, 0))],
        out_specs=pl.BlockSpec((1, r, c), lambda h: (h, 0, 0)),
        out_shape=jax.ShapeDtypeStruct((DA_HEADS, r, c), F32),
        compiler_params=_cparams("arbitrary"),
        name="bias_tiles",
    )(rel_bias, n)


def _proj_body(x_ref, nw_ref, w_ref, q_ref, k_ref, v_ref, kb_ref, vb_ref, hg_ref):
    hb = _rms(x_ref[...], nw_ref[...]).astype(BF16)
    q_ref[...] = (_dot(hb, w_ref[:, 0:512]) * (DA_QK ** -0.5)).astype(BF16)
    tm = x_ref.shape[0]
    k = _dot(hb, w_ref[:, 512:1024])
    kb_ref[...] = k.astype(BF16)
    v = _dot(hb, w_ref[:, 1024:1536])
    vb_ref[...] = v.astype(BF16)
    for h in range(DA_HEADS):
        k_ref[pl.ds(h, tm, stride=DA_HEADS), :] = k[:, h * DA_V:(h + 1) * DA_V]
        v_ref[pl.ds(h, tm, stride=DA_HEADS), :] = v[:, h * DA_V:(h + 1) * DA_V]
    hg_ref[...] = _dot(hb, w_ref[:, 1536:IN_TOTAL])


def _proj(x, norm_w, w_in_b, tm):
    t = x.shape[0]
    row = lambda i: (i, 0)
    const = lambda i: (0, 0)
    return pl.pallas_call(
        _proj_body,
        grid=(t // tm,),
        in_specs=[pl.BlockSpec((tm, D_MODEL), row),
                  pl.BlockSpec((1, D_MODEL), const),
                  pl.BlockSpec((D_MODEL, IN_TOTAL), const)],
        out_specs=[pl.BlockSpec((tm, 512), row),
                   pl.BlockSpec((tm * DA_HEADS, DA_V), row),
                   pl.BlockSpec((tm * DA_HEADS, DA_V), row),
                   pl.BlockSpec((tm, 512), row),
                   pl.BlockSpec((tm, 512), row),
                   pl.BlockSpec((tm, 2048), row)],
        out_shape=[jax.ShapeDtypeStruct((t, 512), BF16),
                   jax.ShapeDtypeStruct((t * DA_HEADS, DA_V), F32),
                   jax.ShapeDtypeStruct((t * DA_HEADS, DA_V), F32),
                   jax.ShapeDtypeStruct((t, 512), BF16),
                   jax.ShapeDtypeStruct((t, 512), BF16),
                   jax.ShapeDtypeStruct((t, 2048), F32)],
        compiler_params=_cparams("arbitrary"),
        name="proj",
    )(x, norm_w, w_in_b)


def _softmax_update(s, v, m_ref, l_ref, acc_ref, idx):
    m_prev = m_ref[idx]
    m_new = jnp.maximum(m_prev, jnp.max(s, axis=-1, keepdims=True))
    alpha = jnp.exp(m_prev - m_new)
    p = jnp.exp(s - m_new)
    l_ref[idx] = alpha * l_ref[idx] + jnp.sum(p, axis=-1, keepdims=True)
    acc_ref[idx] = alpha * acc_ref[idx] + _dot(p.astype(BF16), v)
    m_ref[idx] = m_new


def _sub_ln(o, w):
    return _rms(o, w) * (1.0 - LAM_INIT)


def _attn_prompt_body(lam_ref, rb_ref, q_ref, k_ref, v_ref, bias_ref, w_ref, o_ref,
                      vt_ref, m_ref, acc_ref):
    h = pl.program_id(1)
    qi = pl.program_id(2)
    tb = ATT_BLOCK
    nblk = vt_ref.shape[0]

    @pl.when(qi == 0)
    def _():
        for j in range(nblk):
            vt_ref[j, 0:DA_V, :] = v_ref[j * tb:(j + 1) * tb, :].T
            vt_ref[j, DA_V:DA_V + SUM_ROWS, :] = jnp.ones((SUM_ROWS, tb), BF16)

    q = q_ref[...]
    lane = lax.broadcasted_iota(jnp.int32, q.shape, 1)
    zero = jnp.zeros_like(q)
    qm = (jnp.where(lane < DA_QK, q, zero), jnp.where(lane >= DA_QK, q, zero))

    m_ref[...] = jnp.full(m_ref.shape, NEG_BIG, F32)
    acc_ref[...] = jnp.zeros(acc_ref.shape, F32)

    def step(first_blk, nblocks, bias, far_bias):
        start = first_blk * tb
        if not isinstance(start, int):
            start = pl.multiple_of(start, tb)
        kb = k_ref[pl.ds(start, nblocks * tb), :]
        for mp in range(2):
            s = _dot_nt(kb, qm[mp])
            m_prev = m_ref[mp]
            if far_bias is None:
                s = s + bias
                m_new = jnp.maximum(m_prev, jnp.max(s, axis=0, keepdims=True))
                shift = m_new
            else:
                m_new = jnp.maximum(m_prev, jnp.max(s, axis=0, keepdims=True) + far_bias)
                shift = m_new - far_bias
            p = jnp.exp(s - shift).astype(BF16)
            pv = _dot(vt_ref[first_blk], p[0:tb])
            for i in range(1, nblocks):
                pv = pv + _dot(vt_ref[first_blk + i], p[i * tb:(i + 1) * tb])
            acc_ref[mp] = jnp.exp(m_prev - m_new) * acc_ref[mp] + pv
            m_ref[mp] = m_new

    far_bias = rb_ref[N_BUCKETS - 1, h]
    n_far = jnp.maximum(qi - 1, 0)
    odd = n_far % 2

    def far_single(_, carry):
        step(0, 1, None, far_bias)
        return carry

    def far_pair(t, carry):
        step(odd + 2 * t, 2, None, far_bias)
        return carry

    lax.fori_loop(0, odd, far_single, 0)
    lax.fori_loop(0, n_far // 2, far_pair, 0)

    first = jnp.maximum(qi - 1, 0)
    off = pl.multiple_of(jnp.where(qi == 0, tb, 0), tb)
    step(first, 2, bias_ref[0, pl.ds(off, 2 * tb), :], None)

    a0 = acc_ref[0]
    a1 = acc_ref[1]
    o = (a0[0:DA_V] / a0[DA_V:DA_V + 1]
         - lam_ref[0] * (a1[0:DA_V] / a1[DA_V:DA_V + 1]))
    o = o * lax.rsqrt(jnp.mean(o * o, axis=0, keepdims=True) + EPS) * w_ref[...]
    o_ref[...] = (o * (1.0 - LAM_INIT)).T.astype(BF16)


def _attn_prompt(lam, rel_bias, qb, kb, vb, bias, subln_col, batch, seq):
    tb = ATT_BLOCK
    assert tb + 1 >= LAST_BUCKET_START, "far blocks must lie entirely in the last T5 bucket"
    nq = seq // tb
    return pl.pallas_call(
        _attn_prompt_body,
        grid=(batch, DA_HEADS, nq),
        in_specs=[pl.BlockSpec(memory_space=pltpu.SMEM),
                  pl.BlockSpec(memory_space=pltpu.SMEM),
                  pl.BlockSpec((tb, DA_V), lambda b, h, i: (b * nq + i, h)),
                  pl.BlockSpec((seq, DA_V), lambda b, h, i: (b, h)),
                  pl.BlockSpec((seq, DA_V), lambda b, h, i: (b, h)),
                  pl.BlockSpec((1, 3 * tb, tb), lambda b, h, i: (h, 0, 0)),
                  pl.BlockSpec((DA_V, 1), lambda b, h, i: (0, 0))],
        out_specs=pl.BlockSpec((tb, DA_V), lambda b, h, i: (b * nq + i, h)),
        out_shape=jax.ShapeDtypeStruct((batch * seq, DA_W), BF16),
        scratch_shapes=[pltpu.VMEM((nq, DA_V + SUM_ROWS, tb), BF16),
                        pltpu.VMEM((2, 1, tb), F32),
                        pltpu.VMEM((2, DA_V + SUM_ROWS, tb), F32)],
        compiler_params=_cparams("arbitrary", "arbitrary", "arbitrary"),
        name="attn_prompt",
    )(lam, rel_bias, qb, kb, vb, bias, subln_col)


def _attn_sample_body(pt_ref, lam_ref, q_ref, bias_ref, kn_ref, vn_ref, w_ref, *rest):
    npg = PAGES_PER_STEP
    k_refs = rest[:npg]
    v_refs = rest[npg:2 * npg]
    o_ref = rest[2 * npg]
    m_ref, l_ref, acc_ref = rest[2 * npg + 1:]
    j = pl.program_id(1)
    last = pl.num_programs(1) - 1
    rows = DA_HEADS * 8

    @pl.when(j == 0)
    def _():
        m_ref[...] = jnp.full(m_ref.shape, NEG_BIG, F32)
        l_ref[...] = jnp.zeros(l_ref.shape, F32)
        acc_ref[...] = jnp.zeros(acc_ref.shape, F32)

    q = q_ref[0]
    lane = lax.broadcasted_iota(jnp.int32, q.shape, 1)
    row = lax.broadcasted_iota(jnp.int32, q.shape, 0)
    w = jnp.where(lane // DA_QK == (row // 4) % 2, q, jnp.zeros_like(q))

    far = bias_ref[0]
    near = bias_ref[1]
    s_parts = []
    for i in range(npg):
        kp = k_refs[i][0].astype(BF16)
        bias = far if i < npg - 1 else jnp.where(j == last, near, far)
        s_parts.append(_dot_nt(w, kp) + bias)
    s = jnp.concatenate(s_parts, axis=1)
    vcat = jnp.concatenate([v_refs[i][0].astype(BF16) for i in range(npg)], axis=0)
    _softmax_update(s, vcat, m_ref, l_ref, acc_ref, 0)

    @pl.when(j == last)
    def _():
        s_new = _dot_nt(w, kn_ref[0]) + bias_ref[2][:, 0:rows]
        _softmax_update(s_new, vn_ref[0], m_ref, l_ref, acc_ref, 0)
        normed = acc_ref[0] / l_ref[0]
        for h in range(DA_HEADS):
            blk = normed[h * 8:(h + 1) * 8, :]
            o = blk[0:4] - lam_ref[0] * blk[4:8]
            o_ref[0, :, h * DA_V:(h + 1) * DA_V] = _sub_ln(o, w_ref[...]).astype(BF16)


def _attn_sample(page_table, lam, q_rows, bias, k_new, v_new, subln_w, cache_k, cache_v):
    nb, n_pages = page_table.shape
    npg = PAGES_PER_STEP
    rows = DA_HEADS * 8
    page_rows = PAGE_SIZE * DA_HEADS

    def page_spec(i):
        return pl.BlockSpec((1, page_rows, DA_V),
                            lambda b, j, pt: (pt[b, j * npg + i], 0, 0))

    seq_spec = lambda shape: pl.BlockSpec(shape, lambda b, j, pt: (b, 0, 0))
    grid_spec = pltpu.PrefetchScalarGridSpec(
        num_scalar_prefetch=1,
        grid=(nb, n_pages // npg),
        in_specs=[pl.BlockSpec(memory_space=pltpu.SMEM),
                  seq_spec((1, rows, DA_V)),
                  pl.BlockSpec((3, rows, page_rows), lambda b, j, pt: (0, 0, 0)),
                  seq_spec((1, rows, DA_V)),
                  seq_spec((1, rows, DA_V)),
                  pl.BlockSpec((1, DA_V), lambda b, j, pt: (0, 0))]
                 + [page_spec(i) for i in range(npg)] * 2,
        out_specs=seq_spec((1, 4, DA_W)),
        scratch_shapes=[pltpu.VMEM((1, rows, 1), F32),
                        pltpu.VMEM((1, rows, 1), F32),
                        pltpu.VMEM((1, rows, DA_V), F32)],
    )
    return pl.pallas_call(
        _attn_sample_body,
        grid_spec=grid_spec,
        out_shape=jax.ShapeDtypeStruct((nb, 4, DA_W), BF16),
        compiler_params=_cparams("arbitrary", "arbitrary"),
        name="attn_sample",
    )(page_table, lam, q_rows, bias, k_new, v_new, subln_w,
      *([cache_k] * npg), *([cache_v] * npg))


def _split3(x):
    hi = x.astype(BF16)
    r = x - hi.astype(F32)
    mid = r.astype(BF16)
    lo = (r - mid.astype(F32)).astype(BF16)
    return jnp.concatenate([hi, mid, lo], axis=1)


def _merge3(p):
    w = p.shape[1] // 3
    return p[:, 0:w] + p[:, w:2 * w] + p[:, 2 * w:3 * w]


def _hgrn_chunks(hg_ref, lb_ref, nw_ref, st_ref, o_ref, chunk, nsub, diag, valid):
    c = chunk
    units = [(s, h) for s in range(nsub) for h in range(HG_HEADS)]
    part = lambda p, u: hg_ref[u[0] * c:(u[0] + 1) * c,
                               p * HG_W + u[1] * HG_K: p * HG_W + (u[1] + 1) * HG_K]
    row = lax.broadcasted_iota(jnp.int32, (c, HG_K), 0)
    r2 = lax.broadcasted_iota(jnp.int32, (c, c), 0)
    c2 = lax.broadcasted_iota(jnp.int32, (c, c), 1)
    tri = jnp.where(c2 <= r2, 1.0, 0.0).astype(BF16)

    q, kk, vb, gc = {}, {}, {}, {}
    for u in units:
        hq = part(0, u)
        lb = lb_ref[:, u[1] * HG_K:(u[1] + 1) * HG_K]
        qu = hq * jax.nn.sigmoid(hq) * (HG_K ** -0.5)
        f = lb + (1.0 - lb) * jax.nn.sigmoid(part(1, u))
        if valid < c:
            f = jnp.where(row < valid, f, 1.0)
            qu = jnp.where(row < valid, qu, 0.0)
        q[u] = qu
        kk[u] = 1.0 - f
        vb[u] = part(2, u).astype(BF16)
        gc[u] = _merge3(_dot(tri, _split3(jnp.log(f))))

    a = {u: jnp.zeros((c, c), F32) for u in units}
    rmod = row % diag
    for d in range(diag):
        for u in units:
            if d == 0:
                x = q[u] * kk[u]
            else:
                e = jnp.where(rmod >= d, gc[u] - pltpu.roll(gc[u], d, axis=0), NEG_BIG)
                x = q[u] * pltpu.roll(kk[u], d, axis=0) * jnp.exp(e)
            a[u] = a[u] + jnp.where(r2 - c2 == d, jnp.sum(x, axis=-1, keepdims=True), 0.0)

    levels = []
    half = diag
    while half < c:
        levels.append(half)
        half *= 2
    if levels:
        gsplit = {u: _split3(gc[u]) for u in units}
        for half in levels:
            sel = jnp.where(c2 == (r2 // (2 * half)) * (2 * half) + half - 1, 1.0, 0.0).astype(BF16)
            second = (row % (2 * half)) >= half
            same = r2 // (2 * half) == c2 // (2 * half)
            for u in units:
                bnd = _merge3(_dot(sel, gsplit[u]))
                qh = q[u] * jnp.exp(jnp.where(second, gc[u] - bnd, NEG_BIG))
                kh = kk[u] * jnp.exp(jnp.where(second, NEG_BIG, bnd - gc[u]))
                al = _dot_nt(qh.astype(BF16), kh.astype(BF16))
                a[u] = a[u] + jnp.where(same, al, 0.0)

    for u in units:
        s, h = u
        st = st_ref[h]
        g_last = gc[u][c - 1:c, :]
        o = (_dot_nt((q[u] * jnp.exp(gc[u])).astype(BF16), st.astype(BF16))
             + _dot(a[u].astype(BF16), vb[u]))
        kl = (kk[u] * jnp.exp(g_last - gc[u])).astype(BF16)
        st_ref[h] = st * jnp.exp(g_last) + _dot_tn(vb[u], kl)
        hgate = part(3, u)
        o = _rms(o, nw_ref[...]) * (hgate * jax.nn.sigmoid(hgate))
        o_ref[s * c:(s + 1) * c, h * HG_V:(h + 1) * HG_V] = o.astype(BF16)


def _hgrn_body(hg_ref, lb_ref, nw_ref, s0_ref, o_ref, sf_ref, st_ref, *, chunk, nsub, diag, valid):
    ci = pl.program_id(1)

    @pl.when(ci == 0)
    def _():
        for h in range(HG_HEADS):
            st_ref[h] = s0_ref[0, h].T

    _hgrn_chunks(hg_ref, lb_ref, nw_ref, st_ref, o_ref, chunk, nsub, diag, valid)

    @pl.when(ci == pl.num_programs(1) - 1)
    def _():
        for h in range(HG_HEADS):
            sf_ref[0, h] = st_ref[h].T


def _hgrn(hg, lb, norm_w, s0, batch, length, chunk, nsub, diag, valid):
    rows = chunk * nsub
    nc = length // rows
    body = functools.partial(_hgrn_body, chunk=chunk, nsub=nsub, diag=diag, valid=valid)
    state_spec = pl.BlockSpec((1, HG_HEADS, HG_K, HG_V), lambda b, c: (b, 0, 0, 0))
    return pl.pallas_call(
        body,
        grid=(batch, nc),
        in_specs=[pl.BlockSpec((rows, 4 * HG_W), lambda b, c: (b * nc + c, 0)),
                  pl.BlockSpec((1, HG_W), lambda b, c: (0, 0)),
                  pl.BlockSpec((1, HG_V), lambda b, c: (0, 0)),
                  state_spec],
        out_specs=[pl.BlockSpec((rows, HG_W), lambda b, c: (b * nc + c, 0)), state_spec],
        out_shape=[jax.ShapeDtypeStruct((batch * length, HG_W), BF16),
                   jax.ShapeDtypeStruct((batch, HG_HEADS, HG_K, HG_V), F32)],
        scratch_shapes=[pltpu.VMEM((HG_HEADS, HG_V, HG_K), F32)],
        compiler_params=_cparams("arbitrary", "arbitrary"),
        name="hgrn",
    )(hg, lb, norm_w, s0)


def _route(logits):
    lane = lax.broadcasted_iota(jnp.int32, logits.shape, 1)
    neg = jnp.float32(-jnp.inf)

    def top(mask):
        masked = jnp.where(mask, logits, neg)
        idx = jnp.argmax(masked, axis=-1, keepdims=True).astype(jnp.int32)
        return jnp.max(masked, axis=-1, keepdims=True), idx

    gmask = lane < N_GROUPS
    gmax, gidx = top(gmask)
    p_top = 1.0 / jnp.sum(jnp.where(gmask, jnp.exp(logits - gmax), 0.0), axis=-1, keepdims=True)
    lo = N_GROUPS + gidx * EXPERTS_PER_GROUP
    emask = (lane >= lo) & (lane < lo + EXPERTS_PER_GROUP)
    t1, i1 = top(emask)
    t2, i2 = top(emask & (lane != i1))
    e2 = jnp.exp(t2 - t1)
    g1 = p_top / (1.0 + e2)
    g2 = p_top * e2 / (1.0 + e2)
    return i1 - N_GROUPS, i2 - N_GROUPS, g1, g2


def _mix_moe_body(x_ref, oda_ref, ohg_ref, wo_ref, n2_ref, wr_ref, br_ref,
                  w1_ref, w3_ref, w2_ref, fw_ref, y_ref, he_ref):
    x1 = (x_ref[...] + _dot(oda_ref[...], wo_ref[0:DA_W, :])
          + _dot(ohg_ref[...], wo_ref[DA_W:DA_W + HG_W, :]))
    h2 = _rms(x1, n2_ref[...])
    hb = h2.astype(BF16)
    hl = (h2 - hb.astype(F32)).astype(BF16)
    logits = (_dot(hb, wr_ref[0]) + _dot(hl, wr_ref[0]) + _dot(hb, wr_ref[1])) + br_ref[...]
    id1, id2, g1, g2 = _route(logits)

    for e in range(N_EXPERTS):
        he = _dot(hb, w1_ref[e])
        he = he * jax.nn.sigmoid(he) * _dot(hb, w3_ref[e])
        c = jnp.where(id1 == e, g1, 0.0) + jnp.where(id2 == e, g2, 0.0)
        he_ref[:, e * D_EXPERT:(e + 1) * D_EXPERT] = (c * he).astype(BF16)
    y = _dot(he_ref[...], w2_ref[...])
    y_ref[...] = _rms(x1 + y, fw_ref[...])


def _mix_moe(x, oda, ohg, wo_b, norm2_w, wr, br, w1_b, w3_b, w2_b, final_w, tm):
    t = x.shape[0]
    row = lambda i: (i, 0)
    c2 = lambda i: (0, 0)
    c3 = lambda i: (0, 0, 0)
    once = pl.Buffered(1)
    return pl.pallas_call(
        _mix_moe_body,
        grid=(t // tm,),
        in_specs=[pl.BlockSpec((tm, D_MODEL), row),
                  pl.BlockSpec((tm, DA_W), row),
                  pl.BlockSpec((tm, HG_W), row),
                  pl.BlockSpec((D_MODEL, D_MODEL), c2, pipeline_mode=once),
                  pl.BlockSpec((1, D_MODEL), c2),
                  pl.BlockSpec((2, D_MODEL, V7X_LANES), c3, pipeline_mode=once),
                  pl.BlockSpec((1, V7X_LANES), c2),
                  pl.BlockSpec((N_EXPERTS, D_MODEL, D_EXPERT), c3, pipeline_mode=once),
                  pl.BlockSpec((N_EXPERTS, D_MODEL, D_EXPERT), c3, pipeline_mode=once),
                  pl.BlockSpec((N_EXPERTS * D_EXPERT, D_MODEL), c2, pipeline_mode=once),
                  pl.BlockSpec((1, D_MODEL), c2)],
        out_specs=pl.BlockSpec((tm, D_MODEL), row),
        out_shape=jax.ShapeDtypeStruct((t, D_MODEL), F32),
        scratch_shapes=[pltpu.VMEM((tm, N_EXPERTS * D_EXPERT), BF16)],
        compiler_params=_cparams("arbitrary"),
        name="mix_moe",
    )(x, oda, ohg, wo_b, norm2_w, wr, br, w1_b, w3_b, w2_b, final_w)


def _prompt_distances():
    tb = ATT_BLOCK
    key = jnp.arange(tb, dtype=jnp.int32)[:, None]
    qry = jnp.arange(tb, dtype=jnp.int32)[None, :]
    return jnp.concatenate([tb + qry - key, qry - key, qry - key - tb], axis=0)


def _sample_distances(dec_seq):
    t = (jnp.arange(8, dtype=jnp.int32) % 4)[:, None]
    key = (jnp.arange(PAGE_SIZE * DA_HEADS, dtype=jnp.int32) // DA_HEADS)[None, :]
    far = jnp.full((8, PAGE_SIZE * DA_HEADS), 1 << 20, jnp.int32)
    near = PAGE_SIZE + t - key
    new = jnp.where(key < dec_seq, t - key, -1)
    return jnp.concatenate([far, near, new], axis=0)


def kernel(x_prompt, x_sample, cache_k, cache_v, state_hgrn, page_table, norm1_w, w_in, rel_bias,
           lambda_q1, lambda_k1, lambda_q2, lambda_k2, da_subln_w, hg_lb_logits, hg_norm_w, w_out,
           norm2_w, router_g_w, router_g_b, router_e_w, router_e_b, expert_w1, expert_w3,
           expert_w2, final_norm_w):
    batch, seq, _ = x_prompt.shape
    nb, dec_seq, _ = x_sample.shape
    assert dec_seq == 4 and seq % ATT_BLOCK == 0 and seq % HG_CHUNK == 0
    l = 0

    lam = (jnp.exp(jnp.sum(lambda_q1[l].astype(F32) * lambda_k1[l].astype(F32)))
           - jnp.exp(jnp.sum(lambda_q2[l].astype(F32) * lambda_k2[l].astype(F32))) + LAM_INIT)
    lam = lam.reshape(1)
    lb = jnp.cumsum(jax.nn.softmax(hg_lb_logits.astype(F32), axis=0), axis=0)[l].reshape(1, HG_W)
    w_in_b = w_in[l].astype(BF16)
    wo_b = w_out[l].astype(BF16)
    w1_b = expert_w1[l].astype(BF16)
    w3_b = expert_w3[l].astype(BF16)
    w2_b = expert_w2[l].astype(BF16).reshape(N_EXPERTS * D_EXPERT, D_MODEL)
    pad = V7X_LANES - N_GROUPS - N_EXPERTS
    wr_f = jnp.pad(jnp.concatenate([router_g_w[l], router_e_w[l]], axis=1), ((0, 0), (0, pad)))
    wr_hi = wr_f.astype(BF16)
    wr = jnp.stack([wr_hi, (wr_f - wr_hi.astype(F32)).astype(BF16)])
    br = jnp.pad(jnp.concatenate([router_g_b[l], router_e_b[l]]), (0, pad)).reshape(1, V7X_LANES)
    n1 = norm1_w[l].reshape(1, D_MODEL)
    n2 = norm2_w[l].reshape(1, D_MODEL)
    fw = final_norm_w.reshape(1, D_MODEL)
    subln = da_subln_w[l].reshape(1, DA_V)
    hgn = hg_norm_w[l].reshape(1, HG_V)

    tp = batch * seq
    qb, k_p, v_p, kb, vb, hg = _proj(x_prompt.reshape(tp, D_MODEL), n1, w_in_b, 512)
    bias_p = _bias_tiles(rel_bias, _prompt_distances(), False)
    oda = _attn_prompt(lam, rel_bias, qb, kb, vb, bias_p, subln.reshape(DA_V, 1), batch, seq)
    s0p = jnp.zeros((batch, HG_HEADS, HG_K, HG_V), F32)
    ohg, s_p = _hgrn(hg, lb, hgn, s0p, batch, seq, HG_CHUNK, HG_SUBCHUNKS, HG_DIAG, HG_CHUNK)
    y_p = _mix_moe(x_prompt.reshape(tp, D_MODEL), oda, ohg, wo_b, n2, wr, br,
                   w1_b, w3_b, w2_b, fw, 512)

    ts = nb * dec_seq
    qs, k_s, v_s, ksb, vsb, hgs = _proj(x_sample.reshape(ts, D_MODEL), n1, w_in_b, ts)
    page_rows = PAGE_SIZE * DA_HEADS
    bias_s = _bias_tiles(rel_bias, _sample_distances(dec_seq), True)
    bias_s = bias_s.reshape(DA_HEADS, 3, 8, page_rows).transpose(1, 0, 2, 3).reshape(
        3, DA_HEADS * 8, page_rows)
    q_rows = jnp.broadcast_to(
        qs.reshape(nb, dec_seq, DA_HEADS, 1, DA_V).transpose(0, 2, 3, 1, 4),
        (nb, DA_HEADS, 2, dec_seq, DA_V)).reshape(nb, DA_HEADS * 8, DA_V)
    new_rows = lambda a: jnp.pad(a.reshape(nb, dec_seq * DA_HEADS, DA_V),
                                 ((0, 0), (0, (8 - dec_seq) * DA_HEADS), (0, 0)))
    n_pool = cache_k.shape[1]
    oda_s = _attn_sample(page_table, lam, q_rows, bias_s, new_rows(ksb), new_rows(vsb), subln,
                         cache_k[l].reshape(n_pool, page_rows, DA_V),
                         cache_v[l].reshape(n_pool, page_rows, DA_V))
    hgs_pad = jnp.pad(hgs.reshape(nb, dec_seq, 4 * HG_W),
                      ((0, 0), (0, SAMPLE_CHUNK - dec_seq), (0, 0)))
    ohg_s, s_s = _hgrn(hgs_pad.reshape(nb * SAMPLE_CHUNK, 4 * HG_W), lb, hgn, state_hgrn[l],
                       nb, SAMPLE_CHUNK, SAMPLE_CHUNK, 1, SAMPLE_CHUNK, dec_seq)
    ohg_s = ohg_s.reshape(nb, SAMPLE_CHUNK, HG_W)[:, :dec_seq].reshape(ts, HG_W)
    y_s = _mix_moe(x_sample.reshape(ts, D_MODEL), oda_s.reshape(ts, DA_W), ohg_s, wo_b, n2, wr, br,
                   w1_b, w3_b, w2_b, fw, ts)

    return (y_p.reshape(batch, seq, D_MODEL),
            y_s.reshape(nb, dec_seq, D_MODEL),
            k_p.reshape(1, batch, seq, DA_HEADS, 2 * DA_QK),
            v_p.reshape(1, batch, seq, DA_HEADS, DA_V),
            s_p.reshape(1, batch, HG_HEADS, HG_K, HG_V),
            k_s.reshape(1, nb, dec_seq, DA_HEADS, 2 * DA_QK),
            v_s.reshape(1, nb, dec_seq, DA_HEADS, DA_V),
            s_s.reshape(1, nb, HG_HEADS, HG_K, HG_V))
```

```python
import functools
import math

import jax
import jax.numpy as jnp
from jax import lax
from jax.experimental import pallas as pl
from jax.experimental.pallas import tpu as pltpu

F32 = jnp.float32
BF16 = jnp.bfloat16

D_MODEL = 1024
PAGE_SIZE = 128
DA_HEADS = 4
DA_QK = 64
DA_V = 128
DA_W = DA_HEADS * DA_V
HG_HEADS = 4
HG_K = 128
HG_V = 128
HG_W = HG_HEADS * HG_V
N_BUCKETS = 32
MAX_DISTANCE = 128
N_GROUPS = 4
EXPERTS_PER_GROUP = 4
N_EXPERTS = N_GROUPS * EXPERTS_PER_GROUP
D_EXPERT = 256
EPS = 1e-6
IN_TOTAL = 7 * 512
LAM_INIT = 0.8 - 0.6 * math.exp(-0.3 * 0)

NEG_BIG = -1e30
V7X_LANES = 128
VMEM_LIMIT = 56 * 1024 * 1024

def _t5_bucket_py(n):
    max_exact = N_BUCKETS // 2
    if n < max_exact:
        return n
    return min(max_exact + int(math.log(n / max_exact) / math.log(MAX_DISTANCE / max_exact)
                               * (N_BUCKETS - max_exact)), N_BUCKETS - 1)


LAST_BUCKET_START = next(n for n in range(MAX_DISTANCE + 1) if _t5_bucket_py(n) == N_BUCKETS - 1)

ATT_BLOCK = 512
SUM_ROWS = 16
HG_CHUNK = 128
HG_SUBCHUNKS = 2
HG_DIAG = 16
SAMPLE_CHUNK = 16
MOE_SLAB = 128
PAGES_PER_STEP = 16
SAMPLE_STREAMS = 1


def _cparams(*sem):
    return pltpu.CompilerParams(dimension_semantics=sem, vmem_limit_bytes=VMEM_LIMIT)


def _dot(a, b):
    return jnp.dot(a, b, preferred_element_type=F32)


def _dot_nt(a, b):
    return lax.dot_general(a, b, (((1,), (1,)), ((), ())), preferred_element_type=F32)


def _dot_tn(a, b):
    return lax.dot_general(a, b, (((0,), (0,)), ((), ())), preferred_element_type=F32)


def _rms(x, w):
    return x * lax.rsqrt(jnp.mean(x * x, axis=-1, keepdims=True) + EPS) * w


def _bias_body(rb_ref, n_ref, o_ref, *, interleaved):
    h = pl.program_id(0)
    n = n_ref[...]
    if interleaved:
        lane = lax.broadcasted_iota(jnp.int32, n.shape, 1)
        n = jnp.where(lane % DA_HEADS == h, n, -1)
    nn = jnp.maximum(n, 0)
    max_exact = N_BUCKETS // 2
    nf = jnp.maximum(nn, max_exact).astype(F32)
    large = max_exact + jnp.floor(jnp.log(nf / max_exact) / math.log(MAX_DISTANCE / max_exact)
                                  * (N_BUCKETS - max_exact)).astype(jnp.int32)
    large = jnp.minimum(large, N_BUCKETS - 1)
    bucket = jnp.where(nn < max_exact, nn, large)
    bias = jnp.zeros(n.shape, F32)
    for b in range(N_BUCKETS):
        bias = jnp.where(bucket == b, rb_ref[b, h], bias)
    o_ref[0] = jnp.where(n >= 0, bias, NEG_BIG)


def _bias_tiles(rel_bias, n, interleaved):
    r, c = n.shape
    return pl.pallas_call(
        functools.partial(_bias_body, interleaved=interleaved),
        grid=(DA_HEADS,),
        in_specs=[pl.BlockSpec(memory_space=pltpu.SMEM),
                  pl.BlockSpec((r, c), lambda h: (0, 0))],
        out_specs=pl.BlockSpec((1, r, c), lambda h: (h, 0, 0)),
        out_shape=jax.ShapeDtypeStruct((DA_HEADS, r, c), F32),
        compiler_params=_cparams("arbitrary"),
        name="bias_tiles",
    )(rel_bias, n)


def _proj_body(x_ref, nw_ref, w_ref, q_ref, k_ref, v_ref, kb_ref, vb_ref, hg_ref):
    hb = _rms(x_ref[...], nw_ref[...]).astype(BF16)
    q_ref[...] = (_dot(hb, w_ref[:, 0:512]) * (DA_QK ** -0.5)).astype(BF16)
    tm = x_ref.shape[0]
    k = _dot(hb, w_ref[:, 512:1024])
    kb_ref[...] = k.astype(BF16)
    v = _dot(hb, w_ref[:, 1024:1536])
    vb_ref[...] = v.astype(BF16)
    for h in range(DA_HEADS):
        k_ref[pl.ds(h, tm, stride=DA_HEADS), :] = k[:, h * DA_V:(h + 1) * DA_V]
        v_ref[pl.ds(h, tm, stride=DA_HEADS), :] = v[:, h * DA_V:(h + 1) * DA_V]
    hg_ref[...] = _dot(hb, w_ref[:, 1536:IN_TOTAL])


def _proj(x, norm_w, w_in_b, tm):
    t = x.shape[0]
    row = lambda i: (i, 0)
    const = lambda i: (0, 0)
    return pl.pallas_call(
        _proj_body,
        grid=(t // tm,),
        in_specs=[pl.BlockSpec((tm, D_MODEL), row),
                  pl.BlockSpec((1, D_MODEL), const),
                  pl.BlockSpec((D_MODEL, IN_TOTAL), const)],
        out_specs=[pl.BlockSpec((tm, 512), row),
                   pl.BlockSpec((tm * DA_HEADS, DA_V), row),
                   pl.BlockSpec((tm * DA_HEADS, DA_V), row),
                   pl.BlockSpec((tm, 512), row),
                   pl.BlockSpec((tm, 512), row),
                   pl.BlockSpec((tm, 2048), row)],
        out_shape=[jax.ShapeDtypeStruct((t, 512), BF16),
                   jax.ShapeDtypeStruct((t * DA_HEADS, DA_V), F32),
                   jax.ShapeDtypeStruct((t * DA_HEADS, DA_V), F32),
                   jax.ShapeDtypeStruct((t, 512), BF16),
                   jax.ShapeDtypeStruct((t, 512), BF16),
                   jax.ShapeDtypeStruct((t, 2048), F32)],
        compiler_params=_cparams("arbitrary"),
        name="proj",
    )(x, norm_w, w_in_b)


def _softmax_update(s, v, m_ref, l_ref, acc_ref, idx):
    m_prev = m_ref[idx]
    m_new = jnp.maximum(m_prev, jnp.max(s, axis=-1, keepdims=True))
    alpha = jnp.exp(m_prev - m_new)
    p = jnp.exp(s - m_new)
    l_ref[idx] = alpha * l_ref[idx] + jnp.sum(p, axis=-1, keepdims=True)
    acc_ref[idx] = alpha * acc_ref[idx] + _dot(p.astype(BF16), v)
    m_ref[idx] = m_new


def _sub_ln(o, w):
    return _rms(o, w) * (1.0 - LAM_INIT)


def _attn_prompt_body(lam_ref, rb_ref, q_ref, k_ref, v_ref, bias_ref, w_ref, o_ref,
                      vt_ref, m_ref, acc_ref):
    h = pl.program_id(1)
    qi = pl.program_id(2)
    tb = ATT_BLOCK
    nblk = vt_ref.shape[0]

    @pl.when(qi == 0)
    def _():
        for j in range(nblk):
            vt_ref[j, 0:DA_V, :] = v_ref[j * tb:(j + 1) * tb, :].T
            vt_ref[j, DA_V:DA_V + SUM_ROWS, :] = jnp.ones((SUM_ROWS, tb), BF16)

    q = q_ref[...]
    lane = lax.broadcasted_iota(jnp.int32, q.shape, 1)
    zero = jnp.zeros_like(q)
    q2 = jnp.concatenate([jnp.where(lane < DA_QK, q, zero), jnp.where(lane >= DA_QK, q, zero)],
                         axis=0)

    m_ref[...] = jnp.full(m_ref.shape, NEG_BIG, F32)
    acc_ref[...] = jnp.zeros(acc_ref.shape, F32)

    def step(first_blk, nblocks, bias, far_bias):
        start = first_blk * tb
        if not isinstance(start, int):
            start = pl.multiple_of(start, tb)
        kb = k_ref[pl.ds(start, nblocks * tb), :]
        s = _dot_nt(kb, q2)
        m_prev = m_ref[...]
        if far_bias is None:
            s = s + jnp.concatenate([bias, bias], axis=1)
            m_new = jnp.maximum(m_prev, jnp.max(s, axis=0, keepdims=True))
            shift = m_new
        else:
            m_new = jnp.maximum(m_prev, jnp.max(s, axis=0, keepdims=True) + far_bias)
            shift = m_new - far_bias
        p = jnp.exp(s - shift).astype(BF16)
        pv = _dot(vt_ref[first_blk], p[0:tb])
        for i in range(1, nblocks):
            pv = pv + _dot(vt_ref[first_blk + i], p[i * tb:(i + 1) * tb])
        acc_ref[...] = jnp.exp(m_prev - m_new) * acc_ref[...] + pv
        m_ref[...] = m_new

    far_bias = rb_ref[N_BUCKETS - 1, h]
    n_far = jnp.maximum(qi - 1, 0)
    odd = n_far % 2

    def far_single(_, carry):
        step(0, 1, None, far_bias)
        return carry

    def far_pair(t, carry):
        step(odd + 2 * t, 2, None, far_bias)
        return carry

    lax.fori_loop(0, odd, far_single, 0)
    lax.fori_loop(0, n_far // 2, far_pair, 0)

    first = jnp.maximum(qi - 1, 0)
    off = pl.multiple_of(jnp.where(qi == 0, tb, 0), tb)
    step(first, 2, bias_ref[0, pl.ds(off, 2 * tb), :], None)

    a0 = acc_ref[:, 0:tb]
    a1 = acc_ref[:, tb:2 * tb]
    o = (a0[0:DA_V] / a0[DA_V:DA_V + 1]
         - lam_ref[0] * (a1[0:DA_V] / a1[DA_V:DA_V + 1]))
    o = o * lax.rsqrt(jnp.mean(o * o, axis=0, keepdims=True) + EPS) * w_ref[...]
    o_ref[...] = (o * (1.0 - LAM_INIT)).T.astype(BF16)


def _attn_prompt(lam, rel_bias, qb, kb, vb, bias, subln_col, batch, seq):
    tb = ATT_BLOCK
    assert tb + 1 >= LAST_BUCKET_START, "far blocks must lie entirely in the last T5 bucket"
    nq = seq // tb
    return pl.pallas_call(
        _attn_prompt_body,
        grid=(batch, DA_HEADS, nq),
        in_specs=[pl.BlockSpec(memory_space=pltpu.SMEM),
                  pl.BlockSpec(memory_space=pltpu.SMEM),
                  pl.BlockSpec((tb, DA_V), lambda b, h, i: (b * nq + i, h)),
                  pl.BlockSpec((seq, DA_V), lambda b, h, i: (b, h)),
                  pl.BlockSpec((seq, DA_V), lambda b, h, i: (b, h)),
                  pl.BlockSpec((1, 3 * tb, tb), lambda b, h, i: (h, 0, 0)),
                  pl.BlockSpec((DA_V, 1), lambda b, h, i: (0, 0))],
        out_specs=pl.BlockSpec((tb, DA_V), lambda b, h, i: (b * nq + i, h)),
        out_shape=jax.ShapeDtypeStruct((batch * seq, DA_W), BF16),
        scratch_shapes=[pltpu.VMEM((nq, DA_V + SUM_ROWS, tb), BF16),
                        pltpu.VMEM((1, 2 * tb), F32),
                        pltpu.VMEM((DA_V + SUM_ROWS, 2 * tb), F32)],
        compiler_params=_cparams("arbitrary", "arbitrary", "arbitrary"),
        name="attn_prompt",
    )(lam, rel_bias, qb, kb, vb, bias, subln_col)


def _attn_sample_body(pt_ref, lam_ref, q_ref, bias_ref, kn_ref, vn_ref, w_ref, *rest):
    npg = PAGES_PER_STEP
    k_refs = rest[:npg]
    v_refs = rest[npg:2 * npg]
    o_ref = rest[2 * npg]
    m_ref, l_ref, acc_ref = rest[2 * npg + 1:]
    j = pl.program_id(1)
    last = pl.num_programs(1) - 1
    rows = DA_HEADS * 8

    @pl.when(j == 0)
    def _():
        m_ref[...] = jnp.full(m_ref.shape, NEG_BIG, F32)
        l_ref[...] = jnp.zeros(l_ref.shape, F32)
        acc_ref[...] = jnp.zeros(acc_ref.shape, F32)

    q = q_ref[0]
    lane = lax.broadcasted_iota(jnp.int32, q.shape, 1)
    row = lax.broadcasted_iota(jnp.int32, q.shape, 0)
    w = jnp.where(lane // DA_QK == (row // 4) % 2, q, jnp.zeros_like(q))

    far = bias_ref[0]
    near = bias_ref[1]
    per = npg // SAMPLE_STREAMS
    for g in range(SAMPLE_STREAMS):
        s_parts = []
        for i in range(g * per, (g + 1) * per):
            kp = k_refs[i][0].astype(BF16)
            bias = far if i < npg - 1 else jnp.where(j == last, near, far)
            s_parts.append(_dot_nt(w, kp) + bias)
        s = jnp.concatenate(s_parts, axis=1)
        vcat = jnp.concatenate([v_refs[i][0].astype(BF16)
                                for i in range(g * per, (g + 1) * per)], axis=0)
        _softmax_update(s, vcat, m_ref, l_ref, acc_ref, g)

    @pl.when(j == last)
    def _():
        s_new = _dot_nt(w, kn_ref[0]) + bias_ref[2][:, 0:rows]
        _softmax_update(s_new, vn_ref[0], m_ref, l_ref, acc_ref, 0)
        m_all = m_ref[0]
        for g in range(1, SAMPLE_STREAMS):
            m_all = jnp.maximum(m_all, m_ref[g])
        l_all = jnp.zeros_like(m_all)
        acc_all = jnp.zeros(acc_ref.shape[1:], F32)
        for g in range(SAMPLE_STREAMS):
            scale = jnp.exp(m_ref[g] - m_all)
            l_all = l_all + scale * l_ref[g]
            acc_all = acc_all + scale * acc_ref[g]
        normed = acc_all / l_all
        for h in range(DA_HEADS):
            blk = normed[h * 8:(h + 1) * 8, :]
            o = blk[0:4] - lam_ref[0] * blk[4:8]
            o_ref[0, :, h * DA_V:(h + 1) * DA_V] = _sub_ln(o, w_ref[...]).astype(BF16)


def _attn_sample(page_table, lam, q_rows, bias, k_new, v_new, subln_w, cache_k, cache_v):
    nb, n_pages = page_table.shape
    npg = PAGES_PER_STEP
    rows = DA_HEADS * 8
    page_rows = PAGE_SIZE * DA_HEADS

    def page_spec(i):
        return pl.BlockSpec((1, page_rows, DA_V),
                            lambda b, j, pt: (pt[b, j * npg + i], 0, 0))

    seq_spec = lambda shape: pl.BlockSpec(shape, lambda b, j, pt: (b, 0, 0))
    grid_spec = pltpu.PrefetchScalarGridSpec(
        num_scalar_prefetch=1,
        grid=(nb, n_pages // npg),
        in_specs=[pl.BlockSpec(memory_space=pltpu.SMEM),
                  seq_spec((1, rows, DA_V)),
                  pl.BlockSpec((3, rows, page_rows), lambda b, j, pt: (0, 0, 0)),
                  seq_spec((1, rows, DA_V)),
                  seq_spec((1, rows, DA_V)),
                  pl.BlockSpec((1, DA_V), lambda b, j, pt: (0, 0))]
                 + [page_spec(i) for i in range(npg)] * 2,
        out_specs=seq_spec((1, 4, DA_W)),
        scratch_shapes=[pltpu.VMEM((SAMPLE_STREAMS, rows, 1), F32),
                        pltpu.VMEM((SAMPLE_STREAMS, rows, 1), F32),
                        pltpu.VMEM((SAMPLE_STREAMS, rows, DA_V), F32)],
    )
    return pl.pallas_call(
        _attn_sample_body,
        grid_spec=grid_spec,
        out_shape=jax.ShapeDtypeStruct((nb, 4, DA_W), BF16),
        compiler_params=_cparams("arbitrary", "arbitrary"),
        name="attn_sample",
    )(page_table, lam, q_rows, bias, k_new, v_new, subln_w,
      *([cache_k] * npg), *([cache_v] * npg))


def _split3(x):
    hi = x.astype(BF16)
    r = x - hi.astype(F32)
    mid = r.astype(BF16)
    lo = (r - mid.astype(F32)).astype(BF16)
    return jnp.concatenate([hi, mid, lo], axis=1)


def _merge3(p):
    w = p.shape[1] // 3
    return p[:, 0:w] + p[:, w:2 * w] + p[:, 2 * w:3 * w]


def _hgrn_chunks(hg_ref, lb_ref, nw_ref, st_ref, o_ref, chunk, nsub, diag, valid):
    c = chunk
    units = [(s, h) for s in range(nsub) for h in range(HG_HEADS)]
    part = lambda p, u: hg_ref[u[0] * c:(u[0] + 1) * c,
                               p * HG_W + u[1] * HG_K: p * HG_W + (u[1] + 1) * HG_K]
    row = lax.broadcasted_iota(jnp.int32, (c, HG_K), 0)
    r2 = lax.broadcasted_iota(jnp.int32, (c, c), 0)
    c2 = lax.broadcasted_iota(jnp.int32, (c, c), 1)
    tri = jnp.where(c2 <= r2, 1.0, 0.0).astype(BF16)

    q, kk, vb, gc = {}, {}, {}, {}
    for u in units:
        hq = part(0, u)
        lb = lb_ref[:, u[1] * HG_K:(u[1] + 1) * HG_K]
        qu = hq * jax.nn.sigmoid(hq) * (HG_K ** -0.5)
        f = lb + (1.0 - lb) * jax.nn.sigmoid(part(1, u))
        if valid < c:
            f = jnp.where(row < valid, f, 1.0)
            qu = jnp.where(row < valid, qu, 0.0)
        q[u] = qu
        kk[u] = 1.0 - f
        vb[u] = part(2, u).astype(BF16)
        gc[u] = _merge3(_dot(tri, _split3(jnp.log(f))))

    a = {u: jnp.zeros((c, c), F32) for u in units}
    rmod = row % diag
    for d in range(diag):
        for u in units:
            if d == 0:
                x = q[u] * kk[u]
            else:
                e = jnp.where(rmod >= d, gc[u] - pltpu.roll(gc[u], d, axis=0), NEG_BIG)
                x = q[u] * pltpu.roll(kk[u], d, axis=0) * jnp.exp(e)
            a[u] = a[u] + jnp.where(r2 - c2 == d, jnp.sum(x, axis=-1, keepdims=True), 0.0)

    levels = []
    half = diag
    while half < c:
        levels.append(half)
        half *= 2
    if levels:
        gsplit = {u: _split3(gc[u]) for u in units}
        for half in levels:
            sel = jnp.where(c2 == (r2 // (2 * half)) * (2 * half) + half - 1, 1.0, 0.0).astype(BF16)
            second = (row % (2 * half)) >= half
            same = r2 // (2 * half) == c2 // (2 * half)
            for u in units:
                bnd = _merge3(_dot(sel, gsplit[u]))
                qh = q[u] * jnp.exp(jnp.where(second, gc[u] - bnd, NEG_BIG))
                kh = kk[u] * jnp.exp(jnp.where(second, NEG_BIG, bnd - gc[u]))
                al = _dot_nt(qh.astype(BF16), kh.astype(BF16))
                a[u] = a[u] + jnp.where(same, al, 0.0)

    for u in units:
        s, h = u
        st = st_ref[h]
        g_last = gc[u][c - 1:c, :]
        o = (_dot_nt((q[u] * jnp.exp(gc[u])).astype(BF16), st.astype(BF16))
             + _dot(a[u].astype(BF16), vb[u]))
        kl = (kk[u] * jnp.exp(g_last - gc[u])).astype(BF16)
        st_ref[h] = st * jnp.exp(g_last) + _dot_tn(vb[u], kl)
        hgate = part(3, u)
        o = _rms(o, nw_ref[...]) * (hgate * jax.nn.sigmoid(hgate))
        o_ref[s * c:(s + 1) * c, h * HG_V:(h + 1) * HG_V] = o.astype(BF16)


def _hgrn_body(hg_ref, lb_ref, nw_ref, s0_ref, o_ref, sf_ref, st_ref, *, chunk, nsub, diag, valid):
    ci = pl.program_id(1)

    @pl.when(ci == 0)
    def _():
        for h in range(HG_HEADS):
            st_ref[h] = s0_ref[0, h].T

    _hgrn_chunks(hg_ref, lb_ref, nw_ref, st_ref, o_ref, chunk, nsub, diag, valid)

    @pl.when(ci == pl.num_programs(1) - 1)
    def _():
        for h in range(HG_HEADS):
            sf_ref[0, h] = st_ref[h].T


def _hgrn(hg, lb, norm_w, s0, batch, length, chunk, nsub, diag, valid):
    rows = chunk * nsub
    nc = length // rows
    body = functools.partial(_hgrn_body, chunk=chunk, nsub=nsub, diag=diag, valid=valid)
    state_spec = pl.BlockSpec((1, HG_HEADS, HG_K, HG_V), lambda b, c: (b, 0, 0, 0))
    return pl.pallas_call(
        body,
        grid=(batch, nc),
        in_specs=[pl.BlockSpec((rows, 4 * HG_W), lambda b, c: (b * nc + c, 0)),
                  pl.BlockSpec((1, HG_W), lambda b, c: (0, 0)),
                  pl.BlockSpec((1, HG_V), lambda b, c: (0, 0)),
                  state_spec],
        out_specs=[pl.BlockSpec((rows, HG_W), lambda b, c: (b * nc + c, 0)), state_spec],
        out_shape=[jax.ShapeDtypeStruct((batch * length, HG_W), BF16),
                   jax.ShapeDtypeStruct((batch, HG_HEADS, HG_K, HG_V), F32)],
        scratch_shapes=[pltpu.VMEM((HG_HEADS, HG_V, HG_K), F32)],
        compiler_params=_cparams("arbitrary", "arbitrary"),
        name="hgrn",
    )(hg, lb, norm_w, s0)


def _route(logits):
    lane = lax.broadcasted_iota(jnp.int32, logits.shape, 1)
    neg = jnp.float32(-jnp.inf)

    def top(mask):
        masked = jnp.where(mask, logits, neg)
        idx = jnp.argmax(masked, axis=-1, keepdims=True).astype(jnp.int32)
        return jnp.max(masked, axis=-1, keepdims=True), idx

    gmask = lane < N_GROUPS
    gmax, gidx = top(gmask)
    p_top = 1.0 / jnp.sum(jnp.where(gmask, jnp.exp(logits - gmax), 0.0), axis=-1, keepdims=True)
    lo = N_GROUPS + gidx * EXPERTS_PER_GROUP
    emask = (lane >= lo) & (lane < lo + EXPERTS_PER_GROUP)
    t1, i1 = top(emask)
    t2, i2 = top(emask & (lane != i1))
    e2 = jnp.exp(t2 - t1)
    g1 = p_top / (1.0 + e2)
    g2 = p_top * e2 / (1.0 + e2)
    return gidx, i1 - N_GROUPS, i2 - N_GROUPS, g1, g2


def _mix_moe_body(x_ref, oda_ref, ohg_ref, wo_ref, n2_ref, wr_ref, br_ref,
                  w1_ref, w3_ref, w2_ref, fw_ref, y_ref, xs_ref, cws_ref, ys_ref):
    tm = x_ref.shape[0]
    x1 = (x_ref[...] + _dot(oda_ref[...], wo_ref[0:DA_W, :])
          + _dot(ohg_ref[...], wo_ref[DA_W:DA_W + HG_W, :]))
    h2 = _rms(x1, n2_ref[...])
    hb = h2.astype(BF16)
    hl = (h2 - hb.astype(F32)).astype(BF16)
    hw = _dot(hb, wr_ref[...])
    logits = (hw[:, 0:V7X_LANES] + hw[:, V7X_LANES:2 * V7X_LANES]
              + _dot(hl, wr_ref[:, 0:V7X_LANES])) + br_ref[...]
    gidx, id1, id2, g1, g2 = _route(logits)

    lane = lax.broadcasted_iota(jnp.int32, logits.shape, 1)
    base = gidx * EXPERTS_PER_GROUP
    cw = jnp.where(lane == id1 - base, g1, 0.0) + jnp.where(lane == id2 - base, g2, 0.0)

    gone = jnp.where(lane == gidx, 1.0, 0.0)
    r2 = lax.broadcasted_iota(jnp.int32, (tm, tm), 0)
    c2 = lax.broadcasted_iota(jnp.int32, (tm, tm), 1)
    before = _dot(jnp.where(c2 < r2, 1.0, 0.0).astype(BF16), gone.astype(BF16))
    sizes_row = before[tm - 1:tm] + gone[tm - 1:tm]
    lane1 = lax.broadcasted_iota(jnp.int32, (1, V7X_LANES), 1)
    sizes = [jnp.sum(jnp.where(lane1 == g, sizes_row, 0.0)).astype(jnp.int32)
             for g in range(N_GROUPS)]
    starts = [jnp.int32(0)]
    for g in range(1, N_GROUPS):
        starts.append(starts[-1] + sizes[g - 1])
    rank = jnp.sum(jnp.where(lane == gidx, before, 0.0), axis=-1, keepdims=True).astype(jnp.int32)
    pos = rank
    for g in range(1, N_GROUPS):
        pos = pos + jnp.where(gidx == g, starts[g], 0)
    perm = jnp.where(c2 == pos, 1.0, 0.0).astype(BF16)

    xs_ref[...] = _dot_tn(perm, hb).astype(BF16)
    cw_hi = cw.astype(BF16)
    cw_lo = (cw - cw_hi.astype(F32)).astype(BF16)
    cws = _dot_tn(perm, jnp.concatenate([cw_hi, cw_lo], axis=1))
    cws_ref[...] = cws[:, 0:V7X_LANES] + cws[:, V7X_LANES:2 * V7X_LANES]
    ys_ref[...] = jnp.zeros(ys_ref.shape, F32)

    rowi = lax.broadcasted_iota(jnp.int32, (MOE_SLAB, 1), 0)
    group_w = EXPERTS_PER_GROUP * D_EXPERT
    for g in range(N_GROUPS):
        lo = starts[g]
        hi = starts[g] + sizes[g]

        def slab(sb, carry, g=g, lo=lo, hi=hi):
            r0 = pl.multiple_of(sb * MOE_SLAB, MOE_SLAB)
            rows = xs_ref[pl.ds(r0, MOE_SLAB), :]
            mine = (r0 + rowi >= lo) & (r0 + rowi < hi)
            cwb = cws_ref[pl.ds(r0, MOE_SLAB), :]
            parts = []
            for j in range(EXPERTS_PER_GROUP):
                e = g * EXPERTS_PER_GROUP + j
                he = _dot(rows, w1_ref[e])
                he = he * jax.nn.sigmoid(he) * _dot(rows, w3_ref[e])
                parts.append((jnp.where(mine, cwb[:, j:j + 1], 0.0) * he).astype(BF16))
            ys_ref[pl.ds(r0, MOE_SLAB), :] += _dot(jnp.concatenate(parts, axis=1),
                                                   w2_ref[g * group_w:(g + 1) * group_w, :])
            return carry

        lax.fori_loop(lo // MOE_SLAB, (hi + MOE_SLAB - 1) // MOE_SLAB, slab, 0)

    y = _dot(perm, ys_ref[...].astype(BF16))
    y_ref[...] = _rms(x1 + y, fw_ref[...])


def _mix_moe(x, oda, ohg, wo_b, norm2_w, wr, br, w1_b, w3_b, w2_b, final_w, tm):
    t = x.shape[0]
    row = lambda i: (i, 0)
    c2 = lambda i: (0, 0)
    c3 = lambda i: (0, 0, 0)
    once = pl.Buffered(1)
    return pl.pallas_call(
        _mix_moe_body,
        grid=(t // tm,),
        in_specs=[pl.BlockSpec((tm, D_MODEL), row),
                  pl.BlockSpec((tm, DA_W), row),
                  pl.BlockSpec((tm, HG_W), row),
                  pl.BlockSpec((D_MODEL, D_MODEL), c2, pipeline_mode=once),
                  pl.BlockSpec((1, D_MODEL), c2),
                  pl.BlockSpec((D_MODEL, 2 * V7X_LANES), c2, pipeline_mode=once),
                  pl.BlockSpec((1, V7X_LANES), c2),
                  pl.BlockSpec((N_EXPERTS, D_MODEL, D_EXPERT), c3, pipeline_mode=once),
                  pl.BlockSpec((N_EXPERTS, D_MODEL, D_EXPERT), c3, pipeline_mode=once),
                  pl.BlockSpec((N_EXPERTS * D_EXPERT, D_MODEL), c2, pipeline_mode=once),
                  pl.BlockSpec((1, D_MODEL), c2)],
        out_specs=pl.BlockSpec((tm, D_MODEL), row),
        out_shape=jax.ShapeDtypeStruct((t, D_MODEL), F32),
        scratch_shapes=[pltpu.VMEM((tm, D_MODEL), BF16),
                        pltpu.VMEM((tm, V7X_LANES), F32),
                        pltpu.VMEM((tm, D_MODEL), F32)],
        compiler_params=_cparams("arbitrary"),
        name="mix_moe",
    )(x, oda, ohg, wo_b, norm2_w, wr, br, w1_b, w3_b, w2_b, final_w)


def _prompt_distances():
    tb = ATT_BLOCK
    key = jnp.arange(tb, dtype=jnp.int32)[:, None]
    qry = jnp.arange(tb, dtype=jnp.int32)[None, :]
    return jnp.concatenate([tb + qry - key, qry - key, qry - key - tb], axis=0)


def _sample_distances(dec_seq):
    t = (jnp.arange(8, dtype=jnp.int32) % 4)[:, None]
    key = (jnp.arange(PAGE_SIZE * DA_HEADS, dtype=jnp.int32) // DA_HEADS)[None, :]
    far = jnp.full((8, PAGE_SIZE * DA_HEADS), 1 << 20, jnp.int32)
    near = PAGE_SIZE + t - key
    new = jnp.where(key < dec_seq, t - key, -1)
    return jnp.concatenate([far, near, new], axis=0)


def kernel(x_prompt, x_sample, cache_k, cache_v, state_hgrn, page_table, norm1_w, w_in, rel_bias,
           lambda_q1, lambda_k1, lambda_q2, lambda_k2, da_subln_w, hg_lb_logits, hg_norm_w, w_out,
           norm2_w, router_g_w, router_g_b, router_e_w, router_e_b, expert_w1, expert_w3,
           expert_w2, final_norm_w):
    batch, seq, _ = x_prompt.shape
    nb, dec_seq, _ = x_sample.shape
    assert dec_seq == 4 and seq % ATT_BLOCK == 0 and seq % HG_CHUNK == 0
    l = 0

    lam = (jnp.exp(jnp.sum(lambda_q1[l].astype(F32) * lambda_k1[l].astype(F32)))
           - jnp.exp(jnp.sum(lambda_q2[l].astype(F32) * lambda_k2[l].astype(F32))) + LAM_INIT)
    lam = lam.reshape(1)
    lb = jnp.cumsum(jax.nn.softmax(hg_lb_logits.astype(F32), axis=0), axis=0)[l].reshape(1, HG_W)
    w_in_b = w_in[l].astype(BF16)
    wo_b = w_out[l].astype(BF16)
    w1_b = expert_w1[l].astype(BF16)
    w3_b = expert_w3[l].astype(BF16)
    w2_b = expert_w2[l].astype(BF16).reshape(N_EXPERTS * D_EXPERT, D_MODEL)
    pad = V7X_LANES - N_GROUPS - N_EXPERTS
    wr_f = jnp.pad(jnp.concatenate([router_g_w[l], router_e_w[l]], axis=1), ((0, 0), (0, pad)))
    wr_hi = wr_f.astype(BF16)
    wr = jnp.concatenate([wr_hi, (wr_f - wr_hi.astype(F32)).astype(BF16)], axis=1)
    br = jnp.pad(jnp.concatenate([router_g_b[l], router_e_b[l]]), (0, pad)).reshape(1, V7X_LANES)
    n1 = norm1_w[l].reshape(1, D_MODEL)
    n2 = norm2_w[l].reshape(1, D_MODEL)
    fw = final_norm_w.reshape(1, D_MODEL)
    subln = da_subln_w[l].reshape(1, DA_V)
    hgn = hg_norm_w[l].reshape(1, HG_V)

    tp = batch * seq
    qb, k_p, v_p, kb, vb, hg = _proj(x_prompt.reshape(tp, D_MODEL), n1, w_in_b, 512)
    bias_p = _bias_tiles(rel_bias, _prompt_distances(), False)
    oda = _attn_prompt(lam, rel_bias, qb, kb, vb, bias_p, subln.reshape(DA_V, 1), batch, seq)
    s0p = jnp.zeros((batch, HG_HEADS, HG_K, HG_V), F32)
    ohg, s_p = _hgrn(hg, lb, hgn, s0p, batch, seq, HG_CHUNK, HG_SUBCHUNKS, HG_DIAG, HG_CHUNK)
    y_p = _mix_moe(x_prompt.reshape(tp, D_MODEL), oda, ohg, wo_b, n2, wr, br,
                   w1_b, w3_b, w2_b, fw, 512)

    ts = nb * dec_seq
    qs, k_s, v_s, ksb, vsb, hgs = _proj(x_sample.reshape(ts, D_MODEL), n1, w_in_b, ts)
    page_rows = PAGE_SIZE * DA_HEADS
    bias_s = _bias_tiles(rel_bias, _sample_distances(dec_seq), True)
    bias_s = bias_s.reshape(DA_HEADS, 3, 8, page_rows).transpose(1, 0, 2, 3).reshape(
        3, DA_HEADS * 8, page_rows)
    q_rows = jnp.broadcast_to(
        qs.reshape(nb, dec_seq, DA_HEADS, 1, DA_V).transpose(0, 2, 3, 1, 4),
        (nb, DA_HEADS, 2, dec_seq, DA_V)).reshape(nb, DA_HEADS * 8, DA_V)
    new_rows = lambda a: jnp.pad(a.reshape(nb, dec_seq * DA_HEADS, DA_V),
                                 ((0, 0), (0, (8 - dec_seq) * DA_HEADS), (0, 0)))
    n_pool = cache_k.shape[1]
    oda_s = _attn_sample(page_table, lam, q_rows, bias_s, new_rows(ksb), new_rows(vsb), subln,
                         cache_k[l].reshape(n_pool, page_rows, DA_V),
                         cache_v[l].reshape(n_pool, page_rows, DA_V))
    hgs_pad = jnp.pad(hgs.reshape(nb, dec_seq, 4 * HG_W),
                      ((0, 0), (0, SAMPLE_CHUNK - dec_seq), (0, 0)))
    ohg_s, s_s = _hgrn(hgs_pad.reshape(nb * SAMPLE_CHUNK, 4 * HG_W), lb, hgn, state_hgrn[l],
                       nb, SAMPLE_CHUNK, SAMPLE_CHUNK, 1, SAMPLE_CHUNK, dec_seq)
    ohg_s = ohg_s.reshape(nb, SAMPLE_CHUNK, HG_W)[:, :dec_seq].reshape(ts, HG_W)
    y_s = _mix_moe(x_sample.reshape(ts, D_MODEL), oda_s.reshape(ts, DA_W), ohg_s, wo_b, n2, wr, br,
                   w1_b, w3_b, w2_b, fw, ts)

    return (y_p.reshape(batch, seq, D_MODEL),
            y_s.reshape(nb, dec_seq, D_MODEL),
            k_p.reshape(1, batch, seq, DA_HEADS, 2 * DA_QK),
            v_p.reshape(1, batch, seq, DA_HEADS, DA_V),
            s_p.reshape(1, batch, HG_HEADS, HG_K, HG_V),
            k_s.reshape(1, nb, dec_seq, DA_HEADS, 2 * DA_QK),
            v_s.reshape(1, nb, dec_seq, DA_HEADS, DA_V),
            s_s.reshape(1, nb, HG_HEADS, HG_K, HG_V))
```

```python
import functools
import math

import jax
import jax.numpy as jnp
from jax import lax
from jax.experimental import pallas as pl
from jax.experimental.pallas import tpu as pltpu

F32 = jnp.float32
BF16 = jnp.bfloat16

D_MODEL = 1024
PAGE_SIZE = 128
DA_HEADS = 4
DA_QK = 64
DA_V = 128
DA_W = DA_HEADS * DA_V
HG_HEADS = 4
HG_K = 128
HG_V = 128
HG_W = HG_HEADS * HG_V
N_BUCKETS = 32
MAX_DISTANCE = 128
N_GROUPS = 4
EXPERTS_PER_GROUP = 4
N_EXPERTS = N_GROUPS * EXPERTS_PER_GROUP
D_EXPERT = 256
EPS = 1e-6
IN_TOTAL = 7 * 512
LAM_INIT = 0.8 - 0.6 * math.exp(-0.3 * 0)

NEG_BIG = -1e30
V7X_LANES = 128
VMEM_LIMIT = 56 * 1024 * 1024

def _t5_bucket_py(n):
    max_exact = N_BUCKETS // 2
    if n < max_exact:
        return n
    return min(max_exact + int(math.log(n / max_exact) / math.log(MAX_DISTANCE / max_exact)
                               * (N_BUCKETS - max_exact)), N_BUCKETS - 1)


LAST_BUCKET_START = next(n for n in range(MAX_DISTANCE + 1) if _t5_bucket_py(n) == N_BUCKETS - 1)

ATT_BLOCK = 512
SUM_ROWS = 16
HG_CHUNK = 128
HG_SUBCHUNKS = 2
HG_DIAG = 16
SAMPLE_CHUNK = 16
MOE_SLAB = 128
PAGES_PER_STEP = 32
SAMPLE_STREAMS = 1


def _cparams(*sem):
    return pltpu.CompilerParams(dimension_semantics=sem, vmem_limit_bytes=VMEM_LIMIT)


def _dot(a, b):
    return jnp.dot(a, b, preferred_element_type=F32)


def _dot_nt(a, b):
    return lax.dot_general(a, b, (((1,), (1,)), ((), ())), preferred_element_type=F32)


def _dot_tn(a, b):
    return lax.dot_general(a, b, (((0,), (0,)), ((), ())), preferred_element_type=F32)


def _rms(x, w):
    return x * lax.rsqrt(jnp.mean(x * x, axis=-1, keepdims=True) + EPS) * w


def _bias_body(rb_ref, n_ref, o_ref, *, interleaved):
    h = pl.program_id(0)
    n = n_ref[...]
    if interleaved:
        lane = lax.broadcasted_iota(jnp.int32, n.shape, 1)
        n = jnp.where(lane % DA_HEADS == h, n, -1)
    nn = jnp.maximum(n, 0)
    max_exact = N_BUCKETS // 2
    nf = jnp.maximum(nn, max_exact).astype(F32)
    large = max_exact + jnp.floor(jnp.log(nf / max_exact) / math.log(MAX_DISTANCE / max_exact)
                                  * (N_BUCKETS - max_exact)).astype(jnp.int32)
    large = jnp.minimum(large, N_BUCKETS - 1)
    bucket = jnp.where(nn < max_exact, nn, large)
    bias = jnp.zeros(n.shape, F32)
    for b in range(N_BUCKETS):
        bias = jnp.where(bucket == b, rb_ref[b, h], bias)
    o_ref[0] = jnp.where(n >= 0, bias, NEG_BIG)


def _bias_tiles(rel_bias, n, interleaved):
    r, c = n.shape
    return pl.pallas_call(
        functools.partial(_bias_body, interleaved=interleaved),
        grid=(DA_HEADS,),
        in_specs=[pl.BlockSpec(memory_space=pltpu.SMEM),
                  pl.BlockSpec((r, c), lambda h: (0, 0))],
        out_specs=pl.BlockSpec((1, r, c), lambda h: (h, 0, 0)),
        out_shape=jax.ShapeDtypeStruct((DA_HEADS, r, c), F32),
        compiler_params=_cparams("arbitrary"),
        name="bias_tiles",
    )(rel_bias, n)


def _proj_body(x_ref, nw_ref, w_ref, q_ref, k_ref, v_ref, kb_ref, vb_ref, hg_ref):
    hb = _rms(x_ref[...], nw_ref[...]).astype(BF16)
    q_ref[...] = (_dot(hb, w_ref[:, 0:512]) * (DA_QK ** -0.5)).astype(BF16)
    tm = x_ref.shape[0]
    k = _dot(hb, w_ref[:, 512:1024])
    kb_ref[...] = k.astype(BF16)
    v = _dot(hb, w_ref[:, 1024:1536])
    vb_ref[...] = v.astype(BF16)
    for h in range(DA_HEADS):
        k_ref[pl.ds(h, tm, stride=DA_HEADS), :] = k[:, h * DA_V:(h + 1) * DA_V]
        v_ref[pl.ds(h, tm, stride=DA_HEADS), :] = v[:, h * DA_V:(h + 1) * DA_V]
    hg_ref[...] = _dot(hb, w_ref[:, 1536:IN_TOTAL])


def _proj(x, norm_w, w_in_b, tm):
    t = x.shape[0]
    row = lambda i: (i, 0)
    const = lambda i: (0, 0)
    return pl.pallas_call(
        _proj_body,
        grid=(t // tm,),
        in_specs=[pl.BlockSpec((tm, D_MODEL), row),
                  pl.BlockSpec((1, D_MODEL), const),
                  pl.BlockSpec((D_MODEL, IN_TOTAL), const)],
        out_specs=[pl.BlockSpec((tm, 512), row),
                   pl.BlockSpec((tm * DA_HEADS, DA_V), row),
                   pl.BlockSpec((tm * DA_HEADS, DA_V), row),
                   pl.BlockSpec((tm, 512), row),
                   pl.BlockSpec((tm, 512), row),
                   pl.BlockSpec((tm, 2048), row)],
        out_shape=[jax.ShapeDtypeStruct((t, 512), BF16),
                   jax.ShapeDtypeStruct((t * DA_HEADS, DA_V), F32),
                   jax.ShapeDtypeStruct((t * DA_HEADS, DA_V), F32),
                   jax.ShapeDtypeStruct((t, 512), BF16),
                   jax.ShapeDtypeStruct((t, 512), BF16),
                   jax.ShapeDtypeStruct((t, 2048), F32)],
        compiler_params=_cparams("arbitrary"),
        name="proj",
    )(x, norm_w, w_in_b)


def _softmax_update(s, v, m_ref, l_ref, acc_ref, idx):
    m_prev = m_ref[idx]
    m_new = jnp.maximum(m_prev, jnp.max(s, axis=-1, keepdims=True))
    alpha = jnp.exp(m_prev - m_new)
    p = jnp.exp(s - m_new)
    l_ref[idx] = alpha * l_ref[idx] + jnp.sum(p, axis=-1, keepdims=True)
    acc_ref[idx] = alpha * acc_ref[idx] + _dot(p.astype(BF16), v)
    m_ref[idx] = m_new


def _sub_ln(o, w):
    return _rms(o, w) * (1.0 - LAM_INIT)


def _attn_prompt_body(lam_ref, rb_ref, q_ref, k_ref, v_ref, bias_ref, w_ref, o_ref,
                      vt_ref, m_ref, acc_ref):
    h = pl.program_id(1)
    qi = pl.program_id(2)
    tb = ATT_BLOCK
    nblk = vt_ref.shape[0]

    @pl.when(qi == 0)
    def _():
        for j in range(nblk):
            vt_ref[j, 0:DA_V, :] = v_ref[j * tb:(j + 1) * tb, :].T
            vt_ref[j, DA_V:DA_V + SUM_ROWS, :] = jnp.ones((SUM_ROWS, tb), BF16)

    q = q_ref[...]
    lane = lax.broadcasted_iota(jnp.int32, q.shape, 1)
    zero = jnp.zeros_like(q)
    q2 = jnp.concatenate([jnp.where(lane < DA_QK, q, zero), jnp.where(lane >= DA_QK, q, zero)],
                         axis=0)

    m_ref[...] = jnp.full(m_ref.shape, NEG_BIG, F32)
    acc_ref[...] = jnp.zeros(acc_ref.shape, F32)

    def step(first_blk, nblocks, bias, far_bias):
        start = first_blk * tb
        if not isinstance(start, int):
            start = pl.multiple_of(start, tb)
        kb = k_ref[pl.ds(start, nblocks * tb), :]
        s = _dot_nt(kb, q2)
        m_prev = m_ref[...]
        if far_bias is None:
            s = s + jnp.concatenate([bias, bias], axis=1)
            m_new = jnp.maximum(m_prev, jnp.max(s, axis=0, keepdims=True))
            shift = m_new
        else:
            m_new = jnp.maximum(m_prev, jnp.max(s, axis=0, keepdims=True) + far_bias)
            shift = m_new - far_bias
        p = jnp.exp(s - shift).astype(BF16)
        pv = _dot(vt_ref[first_blk], p[0:tb])
        for i in range(1, nblocks):
            pv = pv + _dot(vt_ref[first_blk + i], p[i * tb:(i + 1) * tb])
        acc_ref[...] = jnp.exp(m_prev - m_new) * acc_ref[...] + pv
        m_ref[...] = m_new

    far_bias = rb_ref[N_BUCKETS - 1, h]
    n_far = jnp.maximum(qi - 1, 0)
    odd = n_far % 2

    def far_single(_, carry):
        step(0, 1, None, far_bias)
        return carry

    def far_pair(t, carry):
        step(odd + 2 * t, 2, None, far_bias)
        return carry

    lax.fori_loop(0, odd, far_single, 0)
    lax.fori_loop(0, n_far // 2, far_pair, 0)

    first = jnp.maximum(qi - 1, 0)
    off = pl.multiple_of(jnp.where(qi == 0, tb, 0), tb)
    step(first, 2, bias_ref[0, pl.ds(off, 2 * tb), :], None)

    a0 = acc_ref[:, 0:tb]
    a1 = acc_ref[:, tb:2 * tb]
    o = (a0[0:DA_V] / a0[DA_V:DA_V + 1]
         - lam_ref[0] * (a1[0:DA_V] / a1[DA_V:DA_V + 1]))
    o = o * lax.rsqrt(jnp.mean(o * o, axis=0, keepdims=True) + EPS) * w_ref[...]
    o_ref[...] = (o * (1.0 - LAM_INIT)).T.astype(BF16)


def _attn_prompt(lam, rel_bias, qb, kb, vb, bias, subln_col, batch, seq):
    tb = ATT_BLOCK
    assert tb + 1 >= LAST_BUCKET_START, "far blocks must lie entirely in the last T5 bucket"
    nq = seq // tb
    return pl.pallas_call(
        _attn_prompt_body,
        grid=(batch, DA_HEADS, nq),
        in_specs=[pl.BlockSpec(memory_space=pltpu.SMEM),
                  pl.BlockSpec(memory_space=pltpu.SMEM),
                  pl.BlockSpec((tb, DA_V), lambda b, h, i: (b * nq + i, h)),
                  pl.BlockSpec((seq, DA_V), lambda b, h, i: (b, h)),
                  pl.BlockSpec((seq, DA_V), lambda b, h, i: (b, h)),
                  pl.BlockSpec((1, 3 * tb, tb), lambda b, h, i: (h, 0, 0)),
                  pl.BlockSpec((DA_V, 1), lambda b, h, i: (0, 0))],
        out_specs=pl.BlockSpec((tb, DA_V), lambda b, h, i: (b * nq + i, h)),
        out_shape=jax.ShapeDtypeStruct((batch * seq, DA_W), BF16),
        scratch_shapes=[pltpu.VMEM((nq, DA_V + SUM_ROWS, tb), BF16),
                        pltpu.VMEM((1, 2 * tb), F32),
                        pltpu.VMEM((DA_V + SUM_ROWS, 2 * tb), F32)],
        compiler_params=_cparams("arbitrary", "arbitrary", "arbitrary"),
        name="attn_prompt",
    )(lam, rel_bias, qb, kb, vb, bias, subln_col)


def _attn_sample_body(pt_ref, lam_ref, q_ref, bias_ref, kn_ref, vn_ref, w_ref, *rest):
    npg = PAGES_PER_STEP
    k_refs = rest[:npg]
    v_refs = rest[npg:2 * npg]
    o_ref = rest[2 * npg]
    m_ref, l_ref, acc_ref = rest[2 * npg + 1:]
    j = pl.program_id(1)
    last = pl.num_programs(1) - 1
    rows = DA_HEADS * 8

    @pl.when(j == 0)
    def _():
        m_ref[...] = jnp.full(m_ref.shape, NEG_BIG, F32)
        l_ref[...] = jnp.zeros(l_ref.shape, F32)
        acc_ref[...] = jnp.zeros(acc_ref.shape, F32)

    q = q_ref[0]
    lane = lax.broadcasted_iota(jnp.int32, q.shape, 1)
    row = lax.broadcasted_iota(jnp.int32, q.shape, 0)
    w = jnp.where(lane // DA_QK == (row // 4) % 2, q, jnp.zeros_like(q))

    far = bias_ref[0]
    near = bias_ref[1]
    per = npg // SAMPLE_STREAMS
    for g in range(SAMPLE_STREAMS):
        pages = range(g * per, (g + 1) * per)
        kcat = jnp.concatenate([k_refs[i][0].astype(BF16) for i in pages], axis=0)
        bias = jnp.concatenate(
            [far if i < npg - 1 else jnp.where(j == last, near, far) for i in pages], axis=1)
        s = _dot_nt(w, kcat) + bias
        vcat = jnp.concatenate([v_refs[i][0].astype(BF16) for i in pages], axis=0)
        _softmax_update(s, vcat, m_ref, l_ref, acc_ref, g)

    @pl.when(j == last)
    def _():
        s_new = _dot_nt(w, kn_ref[0]) + bias_ref[2][:, 0:rows]
        _softmax_update(s_new, vn_ref[0], m_ref, l_ref, acc_ref, 0)
        m_all = m_ref[0]
        for g in range(1, SAMPLE_STREAMS):
            m_all = jnp.maximum(m_all, m_ref[g])
        l_all = jnp.zeros_like(m_all)
        acc_all = jnp.zeros(acc_ref.shape[1:], F32)
        for g in range(SAMPLE_STREAMS):
            scale = jnp.exp(m_ref[g] - m_all)
            l_all = l_all + scale * l_ref[g]
            acc_all = acc_all + scale * acc_ref[g]
        normed = acc_all / l_all
        for h in range(DA_HEADS):
            blk = normed[h * 8:(h + 1) * 8, :]
            o = blk[0:4] - lam_ref[0] * blk[4:8]
            o_ref[0, :, h * DA_V:(h + 1) * DA_V] = _sub_ln(o, w_ref[...]).astype(BF16)


def _attn_sample(page_table, lam, q_rows, bias, k_new, v_new, subln_w, cache_k, cache_v):
    nb, n_pages = page_table.shape
    npg = PAGES_PER_STEP
    rows = DA_HEADS * 8
    page_rows = PAGE_SIZE * DA_HEADS

    def page_spec(i):
        return pl.BlockSpec((1, page_rows, DA_V),
                            lambda b, j, pt: (pt[b, j * npg + i], 0, 0))

    seq_spec = lambda shape: pl.BlockSpec(shape, lambda b, j, pt: (b, 0, 0))
    grid_spec = pltpu.PrefetchScalarGridSpec(
        num_scalar_prefetch=1,
        grid=(nb, n_pages // npg),
        in_specs=[pl.BlockSpec(memory_space=pltpu.SMEM),
                  seq_spec((1, rows, DA_V)),
                  pl.BlockSpec((3, rows, page_rows), lambda b, j, pt: (0, 0, 0)),
                  seq_spec((1, rows, DA_V)),
                  seq_spec((1, rows, DA_V)),
                  pl.BlockSpec((1, DA_V), lambda b, j, pt: (0, 0))]
                 + [page_spec(i) for i in range(npg)] * 2,
        out_specs=seq_spec((1, 4, DA_W)),
        scratch_shapes=[pltpu.VMEM((SAMPLE_STREAMS, rows, 1), F32),
                        pltpu.VMEM((SAMPLE_STREAMS, rows, 1), F32),
                        pltpu.VMEM((SAMPLE_STREAMS, rows, DA_V), F32)],
    )
    return pl.pallas_call(
        _attn_sample_body,
        grid_spec=grid_spec,
        out_shape=jax.ShapeDtypeStruct((nb, 4, DA_W), BF16),
        compiler_params=_cparams("arbitrary", "arbitrary"),
        name="attn_sample",
    )(page_table, lam, q_rows, bias, k_new, v_new, subln_w,
      *([cache_k] * npg), *([cache_v] * npg))


def _split3(x):
    hi = x.astype(BF16)
    r = x - hi.astype(F32)
    mid = r.astype(BF16)
    lo = (r - mid.astype(F32)).astype(BF16)
    return jnp.concatenate([hi, mid, lo], axis=1)


def _merge3(p):
    w = p.shape[1] // 3
    return p[:, 0:w] + p[:, w:2 * w] + p[:, 2 * w:3 * w]


def _hgrn_chunks(hg_ref, lb_ref, nw_ref, st_ref, o_ref, chunk, nsub, diag, valid):
    c = chunk
    units = [(s, h) for s in range(nsub) for h in range(HG_HEADS)]
    part = lambda p, u: hg_ref[u[0] * c:(u[0] + 1) * c,
                               p * HG_W + u[1] * HG_K: p * HG_W + (u[1] + 1) * HG_K]
    row = lax.broadcasted_iota(jnp.int32, (c, HG_K), 0)
    r2 = lax.broadcasted_iota(jnp.int32, (c, c), 0)
    c2 = lax.broadcasted_iota(jnp.int32, (c, c), 1)
    tri = jnp.where(c2 <= r2, 1.0, 0.0).astype(BF16)

    q, kk, vb, gc = {}, {}, {}, {}
    for u in units:
        hq = part(0, u)
        lb = lb_ref[:, u[1] * HG_K:(u[1] + 1) * HG_K]
        qu = hq * jax.nn.sigmoid(hq) * (HG_K ** -0.5)
        f = lb + (1.0 - lb) * jax.nn.sigmoid(part(1, u))
        if valid < c:
            f = jnp.where(row < valid, f, 1.0)
            qu = jnp.where(row < valid, qu, 0.0)
        q[u] = qu
        kk[u] = 1.0 - f
        vb[u] = part(2, u).astype(BF16)
        gc[u] = _merge3(_dot(tri, _split3(jnp.log2(f))))

    packed = c == V7X_LANES
    a = {u: jnp.zeros((c, c), F32) for u in units}
    rmod = row % diag
    for d in range(diag):
        for u in units:
            if d == 0:
                x = q[u] * kk[u]
            else:
                e = jnp.where(rmod >= d, gc[u] - pltpu.roll(gc[u], d, axis=0), NEG_BIG)
                x = q[u] * pltpu.roll(kk[u], d, axis=0) * jnp.exp2(e)
            xsum = jnp.sum(x, axis=-1, keepdims=True)
            if packed:
                a[u] = jnp.where(c2 == (c - d) % c, xsum, a[u])
            else:
                a[u] = a[u] + jnp.where(r2 - c2 == d, xsum, 0.0)
    if packed:
        for u in units:
            a[u] = pltpu.roll(a[u], 0, axis=1, stride=1, stride_axis=0)

    levels = []
    half = diag
    while half < c:
        levels.append(half)
        half *= 2
    if levels:
        gsplit = {u: _split3(gc[u]) for u in units}
        for half in levels:
            sel = jnp.where(c2 == (r2 // (2 * half)) * (2 * half) + half - 1, 1.0, 0.0).astype(BF16)
            second = (row % (2 * half)) >= half
            same = r2 // (2 * half) == c2 // (2 * half)
            for u in units:
                bnd = _merge3(_dot(sel, gsplit[u]))
                qh = q[u] * jnp.exp2(jnp.where(second, gc[u] - bnd, NEG_BIG))
                kh = kk[u] * jnp.exp2(jnp.where(second, NEG_BIG, bnd - gc[u]))
                al = _dot_nt(qh.astype(BF16), kh.astype(BF16))
                a[u] = a[u] + jnp.where(same, al, 0.0)

    for u in units:
        s, h = u
        st = st_ref[h]
        g_last = gc[u][c - 1:c, :]
        o = (_dot_nt((q[u] * jnp.exp2(gc[u])).astype(BF16), st.astype(BF16))
             + _dot(a[u].astype(BF16), vb[u]))
        kl = (kk[u] * jnp.exp2(g_last - gc[u])).astype(BF16)
        st_ref[h] = st * jnp.exp2(g_last) + _dot_tn(vb[u], kl)
        hgate = part(3, u)
        o = _rms(o, nw_ref[...]) * (hgate * jax.nn.sigmoid(hgate))
        o_ref[s * c:(s + 1) * c, h * HG_V:(h + 1) * HG_V] = o.astype(BF16)


def _hgrn_body(hg_ref, lb_ref, nw_ref, s0_ref, o_ref, sf_ref, st_ref, *, chunk, nsub, diag, valid):
    ci = pl.program_id(1)

    @pl.when(ci == 0)
    def _():
        for h in range(HG_HEADS):
            st_ref[h] = s0_ref[0, h].T

    _hgrn_chunks(hg_ref, lb_ref, nw_ref, st_ref, o_ref, chunk, nsub, diag, valid)

    @pl.when(ci == pl.num_programs(1) - 1)
    def _():
        for h in range(HG_HEADS):
            sf_ref[0, h] = st_ref[h].T


def _hgrn(hg, lb, norm_w, s0, batch, length, chunk, nsub, diag, valid):
    rows = chunk * nsub
    nc = length // rows
    body = functools.partial(_hgrn_body, chunk=chunk, nsub=nsub, diag=diag, valid=valid)
    state_spec = pl.BlockSpec((1, HG_HEADS, HG_K, HG_V), lambda b, c: (b, 0, 0, 0))
    return pl.pallas_call(
        body,
        grid=(batch, nc),
        in_specs=[pl.BlockSpec((rows, 4 * HG_W), lambda b, c: (b * nc + c, 0)),
                  pl.BlockSpec((1, HG_W), lambda b, c: (0, 0)),
                  pl.BlockSpec((1, HG_V), lambda b, c: (0, 0)),
                  state_spec],
        out_specs=[pl.BlockSpec((rows, HG_W), lambda b, c: (b * nc + c, 0)), state_spec],
        out_shape=[jax.ShapeDtypeStruct((batch * length, HG_W), BF16),
                   jax.ShapeDtypeStruct((batch, HG_HEADS, HG_K, HG_V), F32)],
        scratch_shapes=[pltpu.VMEM((HG_HEADS, HG_V, HG_K), F32)],
        compiler_params=_cparams("arbitrary", "arbitrary"),
        name="hgrn",
    )(hg, lb, norm_w, s0)


def _route(logits):
    lane = lax.broadcasted_iota(jnp.int32, logits.shape, 1)
    neg = jnp.float32(-jnp.inf)

    def top(mask):
        masked = jnp.where(mask, logits, neg)
        idx = jnp.argmax(masked, axis=-1, keepdims=True).astype(jnp.int32)
        return jnp.max(masked, axis=-1, keepdims=True), idx

    gmask = lane < N_GROUPS
    gmax, gidx = top(gmask)
    p_top = 1.0 / jnp.sum(jnp.where(gmask, jnp.exp(logits - gmax), 0.0), axis=-1, keepdims=True)
    lo = N_GROUPS + gidx * EXPERTS_PER_GROUP
    emask = (lane >= lo) & (lane < lo + EXPERTS_PER_GROUP)
    t1, i1 = top(emask)
    t2, i2 = top(emask & (lane != i1))
    e2 = jnp.exp(t2 - t1)
    g1 = p_top / (1.0 + e2)
    g2 = p_top * e2 / (1.0 + e2)
    return gidx, i1 - N_GROUPS, i2 - N_GROUPS, g1, g2


def _mix_moe_body(x_ref, oda_ref, ohg_ref, wo_ref, n2_ref, wr_ref, br_ref,
                  w1_ref, w3_ref, w2_ref, fw_ref, y_ref, xs_ref, cws_ref, ys_ref):
    tm = x_ref.shape[0]
    x1 = (x_ref[...] + _dot(oda_ref[...], wo_ref[0:DA_W, :])
          + _dot(ohg_ref[...], wo_ref[DA_W:DA_W + HG_W, :]))
    h2 = _rms(x1, n2_ref[...])
    hb = h2.astype(BF16)
    hl = (h2 - hb.astype(F32)).astype(BF16)
    hw = _dot(hb, wr_ref[...])
    logits = (hw[:, 0:V7X_LANES] + hw[:, V7X_LANES:2 * V7X_LANES]
              + _dot(hl, wr_ref[:, 0:V7X_LANES])) + br_ref[...]
    gidx, id1, id2, g1, g2 = _route(logits)

    lane = lax.broadcasted_iota(jnp.int32, logits.shape, 1)
    base = gidx * EXPERTS_PER_GROUP
    cw = jnp.where(lane == id1 - base, g1, 0.0) + jnp.where(lane == id2 - base, g2, 0.0)

    gone = jnp.where(lane == gidx, 1.0, 0.0)
    r2 = lax.broadcasted_iota(jnp.int32, (tm, tm), 0)
    c2 = lax.broadcasted_iota(jnp.int32, (tm, tm), 1)
    before = _dot(jnp.where(c2 < r2, 1.0, 0.0).astype(BF16), gone.astype(BF16))
    sizes_row = before[tm - 1:tm] + gone[tm - 1:tm]
    lane1 = lax.broadcasted_iota(jnp.int32, (1, V7X_LANES), 1)
    sizes = [jnp.sum(jnp.where(lane1 == g, sizes_row, 0.0)).astype(jnp.int32)
             for g in range(N_GROUPS)]
    starts = [jnp.int32(0)]
    for g in range(1, N_GROUPS):
        starts.append(starts[-1] + sizes[g - 1])
    rank = jnp.sum(jnp.where(lane == gidx, before, 0.0), axis=-1, keepdims=True).astype(jnp.int32)
    pos = rank
    for g in range(1, N_GROUPS):
        pos = pos + jnp.where(gidx == g, starts[g], 0)
    perm = jnp.where(c2 == pos, 1.0, 0.0).astype(BF16)

    xs_ref[...] = _dot_tn(perm, hb).astype(BF16)
    cw_hi = cw.astype(BF16)
    cw_lo = (cw - cw_hi.astype(F32)).astype(BF16)
    cws = _dot_tn(perm, jnp.concatenate([cw_hi, cw_lo], axis=1))
    cws_ref[...] = cws[:, 0:V7X_LANES] + cws[:, V7X_LANES:2 * V7X_LANES]
    ys_ref[...] = jnp.zeros(ys_ref.shape, F32)

    rowi = lax.broadcasted_iota(jnp.int32, (MOE_SLAB, 1), 0)
    group_w = EXPERTS_PER_GROUP * D_EXPERT
    for g in range(N_GROUPS):
        lo = starts[g]
        hi = starts[g] + sizes[g]

        def slab(sb, carry, g=g, lo=lo, hi=hi):
            r0 = pl.multiple_of(sb * MOE_SLAB, MOE_SLAB)
            rows = xs_ref[pl.ds(r0, MOE_SLAB), :]
            mine = (r0 + rowi >= lo) & (r0 + rowi < hi)
            cwb = cws_ref[pl.ds(r0, MOE_SLAB), :]
            parts = []
            for j in range(EXPERTS_PER_GROUP):
                e = g * EXPERTS_PER_GROUP + j
                he = _dot(rows, w1_ref[e])
                he = he * jax.nn.sigmoid(he) * _dot(rows, w3_ref[e])
                parts.append((jnp.where(mine, cwb[:, j:j + 1], 0.0) * he).astype(BF16))
            ys_ref[pl.ds(r0, MOE_SLAB), :] += _dot(jnp.concatenate(parts, axis=1),
                                                   w2_ref[g * group_w:(g + 1) * group_w, :])
            return carry

        lax.fori_loop(lo // MOE_SLAB, (hi + MOE_SLAB - 1) // MOE_SLAB, slab, 0)

    y = _dot(perm, ys_ref[...].astype(BF16))
    y_ref[...] = _rms(x1 + y, fw_ref[...])


def _mix_moe(x, oda, ohg, wo_b, norm2_w, wr, br, w1_b, w3_b, w2_b, final_w, tm):
    t = x.shape[0]
    row = lambda i: (i, 0)
    c2 = lambda i: (0, 0)
    c3 = lambda i: (0, 0, 0)
    once = pl.Buffered(1)
    return pl.pallas_call(
        _mix_moe_body,
        grid=(t // tm,),
        in_specs=[pl.BlockSpec((tm, D_MODEL), row),
                  pl.BlockSpec((tm, DA_W), row),
                  pl.BlockSpec((tm, HG_W), row),
                  pl.BlockSpec((D_MODEL, D_MODEL), c2, pipeline_mode=once),
                  pl.BlockSpec((1, D_MODEL), c2),
                  pl.BlockSpec((D_MODEL, 2 * V7X_LANES), c2, pipeline_mode=once),
                  pl.BlockSpec((1, V7X_LANES), c2),
                  pl.BlockSpec((N_EXPERTS, D_MODEL, D_EXPERT), c3, pipeline_mode=once),
                  pl.BlockSpec((N_EXPERTS, D_MODEL, D_EXPERT), c3, pipeline_mode=once),
                  pl.BlockSpec((N_EXPERTS * D_EXPERT, D_MODEL), c2, pipeline_mode=once),
                  pl.BlockSpec((1, D_MODEL), c2)],
        out_specs=pl.BlockSpec((tm, D_MODEL), row),
        out_shape=jax.ShapeDtypeStruct((t, D_MODEL), F32),
        scratch_shapes=[pltpu.VMEM((tm, D_MODEL), BF16),
                        pltpu.VMEM((tm, V7X_LANES), F32),
                        pltpu.VMEM((tm, D_MODEL), F32)],
        compiler_params=_cparams("arbitrary"),
        name="mix_moe",
    )(x, oda, ohg, wo_b, norm2_w, wr, br, w1_b, w3_b, w2_b, final_w)


def _prompt_distances():
    tb = ATT_BLOCK
    key = jnp.arange(tb, dtype=jnp.int32)[:, None]
    qry = jnp.arange(tb, dtype=jnp.int32)[None, :]
    return jnp.concatenate([tb + qry - key, qry - key, qry - key - tb], axis=0)


def _sample_distances(dec_seq):
    t = (jnp.arange(8, dtype=jnp.int32) % 4)[:, None]
    key = (jnp.arange(PAGE_SIZE * DA_HEADS, dtype=jnp.int32) // DA_HEADS)[None, :]
    far = jnp.full((8, PAGE_SIZE * DA_HEADS), 1 << 20, jnp.int32)
    near = PAGE_SIZE + t - key
    new = jnp.where(key < dec_seq, t - key, -1)
    return jnp.concatenate([far, near, new], axis=0)


def kernel(x_prompt, x_sample, cache_k, cache_v, state_hgrn, page_table, norm1_w, w_in, rel_bias,
           lambda_q1, lambda_k1, lambda_q2, lambda_k2, da_subln_w, hg_lb_logits, hg_norm_w, w_out,
           norm2_w, router_g_w, router_g_b, router_e_w, router_e_b, expert_w1, expert_w3,
           expert_w2, final_norm_w):
    batch, seq, _ = x_prompt.shape
    nb, dec_seq, _ = x_sample.shape
    assert dec_seq == 4 and seq % ATT_BLOCK == 0 and seq % HG_CHUNK == 0
    l = 0

    lam = (jnp.exp(jnp.sum(lambda_q1[l].astype(F32) * lambda_k1[l].astype(F32)))
           - jnp.exp(jnp.sum(lambda_q2[l].astype(F32) * lambda_k2[l].astype(F32))) + LAM_INIT)
    lam = lam.reshape(1)
    lb = jnp.cumsum(jax.nn.softmax(hg_lb_logits.astype(F32), axis=0), axis=0)[l].reshape(1, HG_W)
    w_in_b = w_in[l].astype(BF16)
    wo_b = w_out[l].astype(BF16)
    w1_b = expert_w1[l].astype(BF16)
    w3_b = expert_w3[l].astype(BF16)
    w2_b = expert_w2[l].astype(BF16).reshape(N_EXPERTS * D_EXPERT, D_MODEL)
    pad = V7X_LANES - N_GROUPS - N_EXPERTS
    wr_f = jnp.pad(jnp.concatenate([router_g_w[l], router_e_w[l]], axis=1), ((0, 0), (0, pad)))
    wr_hi = wr_f.astype(BF16)
    wr = jnp.concatenate([wr_hi, (wr_f - wr_hi.astype(F32)).astype(BF16)], axis=1)
    br = jnp.pad(jnp.concatenate([router_g_b[l], router_e_b[l]]), (0, pad)).reshape(1, V7X_LANES)
    n1 = norm1_w[l].reshape(1, D_MODEL)
    n2 = norm2_w[l].reshape(1, D_MODEL)
    fw = final_norm_w.reshape(1, D_MODEL)
    subln = da_subln_w[l].reshape(1, DA_V)
    hgn = hg_norm_w[l].reshape(1, HG_V)

    tp = batch * seq
    qb, k_p, v_p, kb, vb, hg = _proj(x_prompt.reshape(tp, D_MODEL), n1, w_in_b, 512)
    bias_p = _bias_tiles(rel_bias, _prompt_distances(), False)
    oda = _attn_prompt(lam, rel_bias, qb, kb, vb, bias_p, subln.reshape(DA_V, 1), batch, seq)
    s0p = jnp.zeros((batch, HG_HEADS, HG_K, HG_V), F32)
    ohg, s_p = _hgrn(hg, lb, hgn, s0p, batch, seq, HG_CHUNK, HG_SUBCHUNKS, HG_DIAG, HG_CHUNK)
    y_p = _mix_moe(x_prompt.reshape(tp, D_MODEL), oda, ohg, wo_b, n2, wr, br,
                   w1_b, w3_b, w2_b, fw, 512)

    ts = nb * dec_seq
    qs, k_s, v_s, ksb, vsb, hgs = _proj(x_sample.reshape(ts, D_MODEL), n1, w_in_b, ts)
    page_rows = PAGE_SIZE * DA_HEADS
    bias_s = _bias_tiles(rel_bias, _sample_distances(dec_seq), True)
    bias_s = bias_s.reshape(DA_HEADS, 3, 8, page_rows).transpose(1, 0, 2, 3).reshape(
        3, DA_HEADS * 8, page_rows)
    q_rows = jnp.broadcast_to(
        qs.reshape(nb, dec_seq, DA_HEADS, 1, DA_V).transpose(0, 2, 3, 1, 4),
        (nb, DA_HEADS, 2, dec_seq, DA_V)).reshape(nb, DA_HEADS * 8, DA_V)
    new_rows = lambda a: jnp.pad(a.reshape(nb, dec_seq * DA_HEADS, DA_V),
                                 ((0, 0), (0, (8 - dec_seq) * DA_HEADS), (0, 0)))
    n_pool = cache_k.shape[1]
    oda_s = _attn_sample(page_table, lam, q_rows, bias_s, new_rows(ksb), new_rows(vsb), subln,
                         cache_k[l].reshape(n_pool, page_rows, DA_V),
                         cache_v[l].reshape(n_pool, page_rows, DA_V))
    hgs_pad = jnp.pad(hgs.reshape(nb, dec_seq, 4 * HG_W),
                      ((0, 0), (0, SAMPLE_CHUNK - dec_seq), (0, 0)))
    ohg_s, s_s = _hgrn(hgs_pad.reshape(nb * SAMPLE_CHUNK, 4 * HG_W), lb, hgn, state_hgrn[l],
                       nb, SAMPLE_CHUNK, SAMPLE_CHUNK, 1, SAMPLE_CHUNK, dec_seq)
    ohg_s = ohg_s.reshape(nb, SAMPLE_CHUNK, HG_W)[:, :dec_seq].reshape(ts, HG_W)
    y_s = _mix_moe(x_sample.reshape(ts, D_MODEL), oda_s.reshape(ts, DA_W), ohg_s, wo_b, n2, wr, br,
                   w1_b, w3_b, w2_b, fw, ts)

    return (y_p.reshape(batch, seq, D_MODEL),
            y_s.reshape(nb, dec_seq, D_MODEL),
            k_p.reshape(1, batch, seq, DA_HEADS, 2 * DA_QK),
            v_p.reshape(1, batch, seq, DA_HEADS, DA_V),
            s_p.reshape(1, batch, HG_HEADS, HG_K, HG_V),
            k_s.reshape(1, nb, dec_seq, DA_HEADS, 2 * DA_QK),
            v_s.reshape(1, nb, dec_seq, DA_HEADS, DA_V),
            s_s.reshape(1, nb, HG_HEADS, HG_K, HG_V))
```

```python
import functools
import math

import jax
import jax.numpy as jnp
from jax import lax
from jax.experimental import pallas as pl
from jax.experimental.pallas import tpu as pltpu

F32 = jnp.float32
BF16 = jnp.bfloat16

D_MODEL = 1024
PAGE_SIZE = 128
DA_HEADS = 4
DA_QK = 64
DA_V = 128
DA_W = DA_HEADS * DA_V
HG_HEADS = 4
HG_K = 128
HG_V = 128
HG_W = HG_HEADS * HG_V
N_BUCKETS = 32
MAX_DISTANCE = 128
N_GROUPS = 4
EXPERTS_PER_GROUP = 4
N_EXPERTS = N_GROUPS * EXPERTS_PER_GROUP
D_EXPERT = 256
EPS = 1e-6
IN_TOTAL = 7 * 512
LAM_INIT = 0.8 - 0.6 * math.exp(-0.3 * 0)

NEG_BIG = -1e30
V7X_LANES = 128
VMEM_LIMIT = 56 * 1024 * 1024

def _t5_bucket_py(n):
    max_exact = N_BUCKETS // 2
    if n < max_exact:
        return n
    return min(max_exact + int(math.log(n / max_exact) / math.log(MAX_DISTANCE / max_exact)
                               * (N_BUCKETS - max_exact)), N_BUCKETS - 1)


LAST_BUCKET_START = next(n for n in range(MAX_DISTANCE + 1) if _t5_bucket_py(n) == N_BUCKETS - 1)

ATT_BLOCK = 512
SUM_ROWS = 16
HG_CHUNK = 128
HG_SUBCHUNKS = 2
HG_DIAG = 16
SAMPLE_CHUNK = 16
MOE_SLAB = 128
PAGES_PER_STEP = 32
SAMPLE_STREAMS = 1


def _cparams(*sem):
    return pltpu.CompilerParams(dimension_semantics=sem, vmem_limit_bytes=VMEM_LIMIT)


def _dot(a, b):
    return jnp.dot(a, b, preferred_element_type=F32)


def _dot_nt(a, b):
    return lax.dot_general(a, b, (((1,), (1,)), ((), ())), preferred_element_type=F32)


def _dot_tn(a, b):
    return lax.dot_general(a, b, (((0,), (0,)), ((), ())), preferred_element_type=F32)


def _rms(x, w):
    return x * lax.rsqrt(jnp.mean(x * x, axis=-1, keepdims=True) + EPS) * w


def _bias_body(rb_ref, n_ref, o_ref, *, interleaved):
    h = pl.program_id(0)
    n = n_ref[...]
    if interleaved:
        lane = lax.broadcasted_iota(jnp.int32, n.shape, 1)
        n = jnp.where(lane % DA_HEADS == h, n, -1)
    nn = jnp.maximum(n, 0)
    max_exact = N_BUCKETS // 2
    nf = jnp.maximum(nn, max_exact).astype(F32)
    large = max_exact + jnp.floor(jnp.log(nf / max_exact) / math.log(MAX_DISTANCE / max_exact)
                                  * (N_BUCKETS - max_exact)).astype(jnp.int32)
    large = jnp.minimum(large, N_BUCKETS - 1)
    bucket = jnp.where(nn < max_exact, nn, large)
    bias = jnp.zeros(n.shape, F32)
    for b in range(N_BUCKETS):
        bias = jnp.where(bucket == b, rb_ref[b, h], bias)
    o_ref[0] = jnp.where(n >= 0, bias, NEG_BIG)


def _bias_tiles(rel_bias, n, interleaved):
    r, c = n.shape
    return pl.pallas_call(
        functools.partial(_bias_body, interleaved=interleaved),
        grid=(DA_HEADS,),
        in_specs=[pl.BlockSpec(memory_space=pltpu.SMEM),
                  pl.BlockSpec((r, c), lambda h: (0, 0))],
        out_specs=pl.BlockSpec((1, r, c), lambda h: (h, 0, 0)),
        out_shape=jax.ShapeDtypeStruct((DA_HEADS, r, c), F32),
        compiler_params=_cparams("arbitrary"),
        name="bias_tiles",
    )(rel_bias, n)


def _proj_body(x_ref, nw_ref, w_ref, q_ref, k_ref, v_ref, kb_ref, vb_ref, hg_ref):
    hb = _rms(x_ref[...], nw_ref[...]).astype(BF16)
    q_ref[...] = (_dot(hb, w_ref[:, 0:512]) * (DA_QK ** -0.5)).astype(BF16)
    tm = x_ref.shape[0]
    k = _dot(hb, w_ref[:, 512:1024])
    kb_ref[...] = k.astype(BF16)
    v = _dot(hb, w_ref[:, 1024:1536])
    vb_ref[...] = v.astype(BF16)
    for h in range(DA_HEADS):
        k_ref[pl.ds(h, tm, stride=DA_HEADS), :] = k[:, h * DA_V:(h + 1) * DA_V]
        v_ref[pl.ds(h, tm, stride=DA_HEADS), :] = v[:, h * DA_V:(h + 1) * DA_V]
    hg_ref[...] = _dot(hb, w_ref[:, 1536:IN_TOTAL])


def _proj(x, norm_w, w_in_b, tm):
    t = x.shape[0]
    row = lambda i: (i, 0)
    const = lambda i: (0, 0)
    return pl.pallas_call(
        _proj_body,
        grid=(t // tm,),
        in_specs=[pl.BlockSpec((tm, D_MODEL), row),
                  pl.BlockSpec((1, D_MODEL), const),
                  pl.BlockSpec((D_MODEL, IN_TOTAL), const)],
        out_specs=[pl.BlockSpec((tm, 512), row),
                   pl.BlockSpec((tm * DA_HEADS, DA_V), row),
                   pl.BlockSpec((tm * DA_HEADS, DA_V), row),
                   pl.BlockSpec((tm, 512), row),
                   pl.BlockSpec((tm, 512), row),
                   pl.BlockSpec((tm, 2048), row)],
        out_shape=[jax.ShapeDtypeStruct((t, 512), BF16),
                   jax.ShapeDtypeStruct((t * DA_HEADS, DA_V), F32),
                   jax.ShapeDtypeStruct((t * DA_HEADS, DA_V), F32),
                   jax.ShapeDtypeStruct((t, 512), BF16),
                   jax.ShapeDtypeStruct((t, 512), BF16),
                   jax.ShapeDtypeStruct((t, 2048), F32)],
        compiler_params=_cparams("arbitrary"),
        name="proj",
    )(x, norm_w, w_in_b)


def _softmax_update(s, v, m_ref, l_ref, acc_ref, idx):
    m_prev = m_ref[idx]
    m_new = jnp.maximum(m_prev, jnp.max(s, axis=-1, keepdims=True))
    alpha = jnp.exp(m_prev - m_new)
    p = jnp.exp(s - m_new)
    l_ref[idx] = alpha * l_ref[idx] + jnp.sum(p, axis=-1, keepdims=True)
    acc_ref[idx] = alpha * acc_ref[idx] + _dot(p.astype(BF16), v)
    m_ref[idx] = m_new


def _sub_ln(o, w):
    return _rms(o, w) * (1.0 - LAM_INIT)


def _attn_prompt_body(lam_ref, rb_ref, q_ref, k_ref, v_ref, bias_ref, w_ref, o_ref,
                      vt_ref, m_ref, acc_ref, s_ref):
    h = pl.program_id(1)
    qi = pl.program_id(2)
    tb = ATT_BLOCK
    nblk = vt_ref.shape[0]

    @pl.when(qi == 0)
    def _():
        for j in range(nblk):
            vt_ref[j, 0:DA_V, :] = v_ref[j * tb:(j + 1) * tb, :].T
            vt_ref[j, DA_V:DA_V + SUM_ROWS, :] = jnp.ones((SUM_ROWS, tb), BF16)

    q = q_ref[...]
    lane = lax.broadcasted_iota(jnp.int32, q.shape, 1)
    zero = jnp.zeros_like(q)
    q2 = jnp.concatenate([jnp.where(lane < DA_QK, q, zero), jnp.where(lane >= DA_QK, q, zero)],
                         axis=0)

    m_ref[...] = jnp.full(m_ref.shape, NEG_BIG, F32)
    acc_ref[...] = jnp.zeros(acc_ref.shape, F32)

    def scores(first_blk, nblocks):
        start = first_blk * tb
        if not isinstance(start, int):
            start = pl.multiple_of(start, tb)
        return _dot_nt(k_ref[pl.ds(start, nblocks * tb), :], q2)

    def fold(s, first_blk, nblocks, bias, far_bias):
        m_prev = m_ref[...]
        if far_bias is None:
            s = s + jnp.concatenate([bias, bias], axis=1)
            m_new = jnp.maximum(m_prev, jnp.max(s, axis=0, keepdims=True))
            shift = m_new
        else:
            m_new = jnp.maximum(m_prev, jnp.max(s, axis=0, keepdims=True) + far_bias)
            shift = m_new - far_bias
        p = jnp.exp(s - shift).astype(BF16)
        pv = _dot(vt_ref[first_blk], p[0:tb])
        for i in range(1, nblocks):
            pv = pv + _dot(vt_ref[first_blk + i], p[i * tb:(i + 1) * tb])
        acc_ref[...] = jnp.exp(m_prev - m_new) * acc_ref[...] + pv
        m_ref[...] = m_new

    far_bias = rb_ref[N_BUCKETS - 1, h]
    n_far = jnp.maximum(qi - 1, 0)
    odd = n_far % 2


    last = n_far // 2
    pair_blk = lambda p: odd + 2 * p

    def near_fold(slot):
        off = pl.multiple_of(jnp.where(qi == 0, tb, 0), tb)
        fold(s_ref[slot], pair_blk(last), 2, bias_ref[0, pl.ds(off, 2 * tb), :], None)

    @pl.when(odd == 0)
    def _():
        s_ref[0] = scores(pair_blk(0), 2)

    @pl.when(odd == 1)
    def _():
        s_single = scores(0, 1)
        s_ref[0] = scores(1, 2)
        fold(s_single, 0, 1, None, far_bias)

    def two_far_pairs(t, carry):
        s_ref[1] = scores(pair_blk(2 * t + 1), 2)
        fold(s_ref[0], pair_blk(2 * t), 2, None, far_bias)
        s_ref[0] = scores(pair_blk(2 * t + 2), 2)
        fold(s_ref[1], pair_blk(2 * t + 1), 2, None, far_bias)
        return carry

    lax.fori_loop(0, last // 2, two_far_pairs, 0)

    @pl.when(last % 2 == 0)
    def _():
        near_fold(0)

    @pl.when(last % 2 == 1)
    def _():
        s_ref[1] = scores(pair_blk(last), 2)
        fold(s_ref[0], pair_blk(last - 1), 2, None, far_bias)
        near_fold(1)

    a0 = acc_ref[:, 0:tb]
    a1 = acc_ref[:, tb:2 * tb]
    o = (a0[0:DA_V] / a0[DA_V:DA_V + 1]
         - lam_ref[0] * (a1[0:DA_V] / a1[DA_V:DA_V + 1]))
    o = o * lax.rsqrt(jnp.mean(o * o, axis=0, keepdims=True) + EPS) * w_ref[...]
    o_ref[...] = (o * (1.0 - LAM_INIT)).T.astype(BF16)


def _attn_prompt(lam, rel_bias, qb, kb, vb, bias, subln_col, batch, seq):
    tb = ATT_BLOCK
    assert tb + 1 >= LAST_BUCKET_START, "far blocks must lie entirely in the last T5 bucket"
    nq = seq // tb
    return pl.pallas_call(
        _attn_prompt_body,
        grid=(batch, DA_HEADS, nq),
        in_specs=[pl.BlockSpec(memory_space=pltpu.SMEM),
                  pl.BlockSpec(memory_space=pltpu.SMEM),
                  pl.BlockSpec((tb, DA_V), lambda b, h, i: (b * nq + i, h)),
                  pl.BlockSpec((seq, DA_V), lambda b, h, i: (b, h)),
                  pl.BlockSpec((seq, DA_V), lambda b, h, i: (b, h)),
                  pl.BlockSpec((1, 3 * tb, tb), lambda b, h, i: (h, 0, 0)),
                  pl.BlockSpec((DA_V, 1), lambda b, h, i: (0, 0))],
        out_specs=pl.BlockSpec((tb, DA_V), lambda b, h, i: (b * nq + i, h)),
        out_shape=jax.ShapeDtypeStruct((batch * seq, DA_W), BF16),
        scratch_shapes=[pltpu.VMEM((nq, DA_V + SUM_ROWS, tb), BF16),
                        pltpu.VMEM((1, 2 * tb), F32),
                        pltpu.VMEM((DA_V + SUM_ROWS, 2 * tb), F32),
                        pltpu.VMEM((2, 2 * tb, 2 * tb), F32)],
        compiler_params=_cparams("arbitrary", "arbitrary", "arbitrary"),
        name="attn_prompt",
    )(lam, rel_bias, qb, kb, vb, bias, subln_col)


def _attn_sample_body(pt_ref, lam_ref, q_ref, bias_ref, kn_ref, vn_ref, w_ref, *rest):
    npg = PAGES_PER_STEP
    k_refs = rest[:npg]
    v_refs = rest[npg:2 * npg]
    o_ref = rest[2 * npg]
    m_ref, l_ref, acc_ref = rest[2 * npg + 1:]
    j = pl.program_id(1)
    last = pl.num_programs(1) - 1
    rows = DA_HEADS * 8

    @pl.when(j == 0)
    def _():
        m_ref[...] = jnp.full(m_ref.shape, NEG_BIG, F32)
        l_ref[...] = jnp.zeros(l_ref.shape, F32)
        acc_ref[...] = jnp.zeros(acc_ref.shape, F32)

    q = q_ref[0]
    lane = lax.broadcasted_iota(jnp.int32, q.shape, 1)
    row = lax.broadcasted_iota(jnp.int32, q.shape, 0)
    w = jnp.where(lane // DA_QK == (row // 4) % 2, q, jnp.zeros_like(q))

    far = bias_ref[0]
    near = bias_ref[1]
    per = npg // SAMPLE_STREAMS
    for g in range(SAMPLE_STREAMS):
        pages = range(g * per, (g + 1) * per)
        kcat = jnp.concatenate([k_refs[i][0].astype(BF16) for i in pages], axis=0)
        bias = jnp.concatenate(
            [far if i < npg - 1 else jnp.where(j == last, near, far) for i in pages], axis=1)
        s = _dot_nt(w, kcat) + bias
        vcat = jnp.concatenate([v_refs[i][0].astype(BF16) for i in pages], axis=0)
        _softmax_update(s, vcat, m_ref, l_ref, acc_ref, g)

    @pl.when(j == last)
    def _():
        s_new = _dot_nt(w, kn_ref[0]) + bias_ref[2][:, 0:rows]
        _softmax_update(s_new, vn_ref[0], m_ref, l_ref, acc_ref, 0)
        m_all = m_ref[0]
        for g in range(1, SAMPLE_STREAMS):
            m_all = jnp.maximum(m_all, m_ref[g])
        l_all = jnp.zeros_like(m_all)
        acc_all = jnp.zeros(acc_ref.shape[1:], F32)
        for g in range(SAMPLE_STREAMS):
            scale = jnp.exp(m_ref[g] - m_all)
            l_all = l_all + scale * l_ref[g]
            acc_all = acc_all + scale * acc_ref[g]
        normed = acc_all / l_all
        for h in range(DA_HEADS):
            blk = normed[h * 8:(h + 1) * 8, :]
            o = blk[0:4] - lam_ref[0] * blk[4:8]
            o_ref[0, :, h * DA_V:(h + 1) * DA_V] = _sub_ln(o, w_ref[...]).astype(BF16)


def _attn_sample(page_table, lam, q_rows, bias, k_new, v_new, subln_w, cache_k, cache_v):
    nb, n_pages = page_table.shape
    npg = PAGES_PER_STEP
    assert n_pages % npg == 0, (n_pages, npg)
    rows = DA_HEADS * 8
    page_rows = PAGE_SIZE * DA_HEADS

    def page_spec(i):
        return pl.BlockSpec((1, page_rows, DA_V),
                            lambda b, j, pt: (pt[b, j * npg + i], 0, 0))

    seq_spec = lambda shape: pl.BlockSpec(shape, lambda b, j, pt: (b, 0, 0))
    grid_spec = pltpu.PrefetchScalarGridSpec(
        num_scalar_prefetch=1,
        grid=(nb, n_pages // npg),
        in_specs=[pl.BlockSpec(memory_space=pltpu.SMEM),
                  seq_spec((1, rows, DA_V)),
                  pl.BlockSpec((3, rows, page_rows), lambda b, j, pt: (0, 0, 0)),
                  seq_spec((1, rows, DA_V)),
                  seq_spec((1, rows, DA_V)),
                  pl.BlockSpec((1, DA_V), lambda b, j, pt: (0, 0))]
                 + [page_spec(i) for i in range(npg)] * 2,
        out_specs=seq_spec((1, 4, DA_W)),
        scratch_shapes=[pltpu.VMEM((SAMPLE_STREAMS, rows, 1), F32),
                        pltpu.VMEM((SAMPLE_STREAMS, rows, 1), F32),
                        pltpu.VMEM((SAMPLE_STREAMS, rows, DA_V), F32)],
    )
    return pl.pallas_call(
        _attn_sample_body,
        grid_spec=grid_spec,
        out_shape=jax.ShapeDtypeStruct((nb, 4, DA_W), BF16),
        compiler_params=_cparams("arbitrary", "arbitrary"),
        name="attn_sample",
    )(page_table, lam, q_rows, bias, k_new, v_new, subln_w,
      *([cache_k] * npg), *([cache_v] * npg))


def _split3(x):
    hi = x.astype(BF16)
    r = x - hi.astype(F32)
    mid = r.astype(BF16)
    lo = (r - mid.astype(F32)).astype(BF16)
    return jnp.concatenate([hi, mid, lo], axis=1)


def _merge3(p):
    w = p.shape[1] // 3
    return p[:, 0:w] + p[:, w:2 * w] + p[:, 2 * w:3 * w]


def _hgrn_chunks(hg_ref, lb_ref, nw_ref, st_ref, o_ref, chunk, nsub, diag, valid):
    c = chunk
    units = [(s, h) for s in range(nsub) for h in range(HG_HEADS)]
    part = lambda p, u: hg_ref[u[0] * c:(u[0] + 1) * c,
                               p * HG_W + u[1] * HG_K: p * HG_W + (u[1] + 1) * HG_K]
    row = lax.broadcasted_iota(jnp.int32, (c, HG_K), 0)
    r2 = lax.broadcasted_iota(jnp.int32, (c, c), 0)
    c2 = lax.broadcasted_iota(jnp.int32, (c, c), 1)
    tri = jnp.where(c2 <= r2, 1.0, 0.0).astype(BF16)

    q, kk, vb, gc = {}, {}, {}, {}
    for u in units:
        hq = part(0, u)
        lb = lb_ref[:, u[1] * HG_K:(u[1] + 1) * HG_K]
        qu = hq * jax.nn.sigmoid(hq) * (HG_K ** -0.5)
        f = lb + (1.0 - lb) * jax.nn.sigmoid(part(1, u))
        if valid < c:
            f = jnp.where(row < valid, f, 1.0)
            qu = jnp.where(row < valid, qu, 0.0)
        q[u] = qu
        kk[u] = 1.0 - f
        vb[u] = part(2, u).astype(BF16)
        gc[u] = _merge3(_dot(tri, _split3(jnp.log2(f))))

    packed = c == V7X_LANES
    a = {u: jnp.zeros((c, c), F32) for u in units}
    rmod = row % diag
    for d in range(diag):
        for u in units:
            if d == 0:
                x = q[u] * kk[u]
            else:
                e = jnp.where(rmod >= d, gc[u] - pltpu.roll(gc[u], d, axis=0), NEG_BIG)
                x = q[u] * pltpu.roll(kk[u], d, axis=0) * jnp.exp2(e)
            xsum = jnp.sum(x, axis=-1, keepdims=True)
            if packed:
                a[u] = jnp.where(c2 == (c - d) % c, xsum, a[u])
            else:
                a[u] = a[u] + jnp.where(r2 - c2 == d, xsum, 0.0)
    if packed:
        for u in units:
            a[u] = pltpu.roll(a[u], 0, axis=1, stride=1, stride_axis=0)

    levels = []
    half = diag
    while half < c:
        levels.append(half)
        half *= 2
    if levels:
        gsplit = {u: _split3(gc[u]) for u in units}
        for half in levels:
            sel = jnp.where(c2 == (r2 // (2 * half)) * (2 * half) + half - 1, 1.0, 0.0).astype(BF16)
            second = (row % (2 * half)) >= half
            same = r2 // (2 * half) == c2 // (2 * half)
            for u in units:
                bnd = _merge3(_dot(sel, gsplit[u]))
                qh = q[u] * jnp.exp2(jnp.where(second, gc[u] - bnd, NEG_BIG))
                kh = kk[u] * jnp.exp2(jnp.where(second, NEG_BIG, bnd - gc[u]))
                al = _dot_nt(qh.astype(BF16), kh.astype(BF16))
                a[u] = a[u] + jnp.where(same, al, 0.0)

    for u in units:
        s, h = u
        st = st_ref[h]
        g_last = gc[u][c - 1:c, :]
        o = (_dot_nt((q[u] * jnp.exp2(gc[u])).astype(BF16), st.astype(BF16))
             + _dot(a[u].astype(BF16), vb[u]))
        kl = (kk[u] * jnp.exp2(g_last - gc[u])).astype(BF16)
        st_ref[h] = st * jnp.exp2(g_last) + _dot_tn(vb[u], kl)
        hgate = part(3, u)
        o = _rms(o, nw_ref[...]) * (hgate * jax.nn.sigmoid(hgate))
        o_ref[s * c:(s + 1) * c, h * HG_V:(h + 1) * HG_V] = o.astype(BF16)


def _hgrn_body(hg_ref, lb_ref, nw_ref, s0_ref, o_ref, sf_ref, st_ref, *, chunk, nsub, diag, valid):
    ci = pl.program_id(1)

    @pl.when(ci == 0)
    def _():
        for h in range(HG_HEADS):
            st_ref[h] = s0_ref[0, h].T

    _hgrn_chunks(hg_ref, lb_ref, nw_ref, st_ref, o_ref, chunk, nsub, diag, valid)

    @pl.when(ci == pl.num_programs(1) - 1)
    def _():
        for h in range(HG_HEADS):
            sf_ref[0, h] = st_ref[h].T


def _hgrn(hg, lb, norm_w, s0, batch, length, chunk, nsub, diag, valid):
    rows = chunk * nsub
    nc = length // rows
    body = functools.partial(_hgrn_body, chunk=chunk, nsub=nsub, diag=diag, valid=valid)
    state_spec = pl.BlockSpec((1, HG_HEADS, HG_K, HG_V), lambda b, c: (b, 0, 0, 0))
    return pl.pallas_call(
        body,
        grid=(batch, nc),
        in_specs=[pl.BlockSpec((rows, 4 * HG_W), lambda b, c: (b * nc + c, 0)),
                  pl.BlockSpec((1, HG_W), lambda b, c: (0, 0)),
                  pl.BlockSpec((1, HG_V), lambda b, c: (0, 0)),
                  state_spec],
        out_specs=[pl.BlockSpec((rows, HG_W), lambda b, c: (b * nc + c, 0)), state_spec],
        out_shape=[jax.ShapeDtypeStruct((batch * length, HG_W), BF16),
                   jax.ShapeDtypeStruct((batch, HG_HEADS, HG_K, HG_V), F32)],
        scratch_shapes=[pltpu.VMEM((HG_HEADS, HG_V, HG_K), F32)],
        compiler_params=_cparams("arbitrary", "arbitrary"),
        name="hgrn",
    )(hg, lb, norm_w, s0)


def _route(logits):
    lane = lax.broadcasted_iota(jnp.int32, logits.shape, 1)
    neg = jnp.float32(-jnp.inf)

    def top(mask):
        masked = jnp.where(mask, logits, neg)
        idx = jnp.argmax(masked, axis=-1, keepdims=True).astype(jnp.int32)
        return jnp.max(masked, axis=-1, keepdims=True), idx

    gmask = lane < N_GROUPS
    gmax, gidx = top(gmask)
    p_top = 1.0 / jnp.sum(jnp.where(gmask, jnp.exp(logits - gmax), 0.0), axis=-1, keepdims=True)
    lo = N_GROUPS + gidx * EXPERTS_PER_GROUP
    emask = (lane >= lo) & (lane < lo + EXPERTS_PER_GROUP)
    t1, i1 = top(emask)
    t2, i2 = top(emask & (lane != i1))
    e2 = jnp.exp(t2 - t1)
    g1 = p_top / (1.0 + e2)
    g2 = p_top * e2 / (1.0 + e2)
    return gidx, i1 - N_GROUPS, i2 - N_GROUPS, g1, g2


def _mix_moe_body(x_ref, oda_ref, ohg_ref, wo_ref, n2_ref, wr_ref, br_ref,
                  w1_ref, w3_ref, w2_ref, fw_ref, y_ref, xs_ref, cws_ref, ys_ref):
    tm = x_ref.shape[0]
    x1 = (x_ref[...] + _dot(oda_ref[...], wo_ref[0:DA_W, :])
          + _dot(ohg_ref[...], wo_ref[DA_W:DA_W + HG_W, :]))
    h2 = _rms(x1, n2_ref[...])
    hb = h2.astype(BF16)
    hl = (h2 - hb.astype(F32)).astype(BF16)
    hw = _dot(hb, wr_ref[...])
    logits = (hw[:, 0:V7X_LANES] + hw[:, V7X_LANES:2 * V7X_LANES]
              + _dot(hl, wr_ref[:, 0:V7X_LANES])) + br_ref[...]
    gidx, id1, id2, g1, g2 = _route(logits)

    lane = lax.broadcasted_iota(jnp.int32, logits.shape, 1)
    base = gidx * EXPERTS_PER_GROUP
    cw = jnp.where(lane == id1 - base, g1, 0.0) + jnp.where(lane == id2 - base, g2, 0.0)

    gone = jnp.where(lane == gidx, 1.0, 0.0)
    r2 = lax.broadcasted_iota(jnp.int32, (tm, tm), 0)
    c2 = lax.broadcasted_iota(jnp.int32, (tm, tm), 1)
    before = _dot(jnp.where(c2 < r2, 1.0, 0.0).astype(BF16), gone.astype(BF16))
    sizes_row = before[tm - 1:tm] + gone[tm - 1:tm]
    lane1 = lax.broadcasted_iota(jnp.int32, (1, V7X_LANES), 1)
    sizes = [jnp.sum(jnp.where(lane1 == g, sizes_row, 0.0)).astype(jnp.int32)
             for g in range(N_GROUPS)]
    starts = [jnp.int32(0)]
    for g in range(1, N_GROUPS):
        starts.append(starts[-1] + sizes[g - 1])
    rank = jnp.sum(jnp.where(lane == gidx, before, 0.0), axis=-1, keepdims=True).astype(jnp.int32)
    pos = rank
    for g in range(1, N_GROUPS):
        pos = pos + jnp.where(gidx == g, starts[g], 0)
    perm = jnp.where(c2 == pos, 1.0, 0.0).astype(BF16)

    xs_ref[...] = _dot_tn(perm, hb).astype(BF16)
    cw_hi = cw.astype(BF16)
    cw_lo = (cw - cw_hi.astype(F32)).astype(BF16)
    cws = _dot_tn(perm, jnp.concatenate([cw_hi, cw_lo], axis=1))
    cws_ref[...] = cws[:, 0:V7X_LANES] + cws[:, V7X_LANES:2 * V7X_LANES]
    ys_ref[...] = jnp.zeros(ys_ref.shape, F32)

    rowi = lax.broadcasted_iota(jnp.int32, (MOE_SLAB, 1), 0)
    group_w = EXPERTS_PER_GROUP * D_EXPERT
    for g in range(N_GROUPS):
        lo = starts[g]
        hi = starts[g] + sizes[g]

        def slab(sb, carry, g=g, lo=lo, hi=hi):
            r0 = pl.multiple_of(sb * MOE_SLAB, MOE_SLAB)
            rows = xs_ref[pl.ds(r0, MOE_SLAB), :]
            mine = (r0 + rowi >= lo) & (r0 + rowi < hi)
            cwb = cws_ref[pl.ds(r0, MOE_SLAB), :]
            parts = []
            for j in range(EXPERTS_PER_GROUP):
                e = g * EXPERTS_PER_GROUP + j
                he = _dot(rows, w1_ref[e])
                he = he * jax.nn.sigmoid(he) * _dot(rows, w3_ref[e])
                parts.append((jnp.where(mine, cwb[:, j:j + 1], 0.0) * he).astype(BF16))
            ys_ref[pl.ds(r0, MOE_SLAB), :] += _dot(jnp.concatenate(parts, axis=1),
                                                   w2_ref[g * group_w:(g + 1) * group_w, :])
            return carry

        lax.fori_loop(lo // MOE_SLAB, (hi + MOE_SLAB - 1) // MOE_SLAB, slab, 0)

    y = _dot(perm, ys_ref[...].astype(BF16))
    y_ref[...] = _rms(x1 + y, fw_ref[...])


def _mix_moe(x, oda, ohg, wo_b, norm2_w, wr, br, w1_b, w3_b, w2_b, final_w, tm):
    t = x.shape[0]
    row = lambda i: (i, 0)
    c2 = lambda i: (0, 0)
    c3 = lambda i: (0, 0, 0)
    once = pl.Buffered(1)
    return pl.pallas_call(
        _mix_moe_body,
        grid=(t // tm,),
        in_specs=[pl.BlockSpec((tm, D_MODEL), row),
                  pl.BlockSpec((tm, DA_W), row),
                  pl.BlockSpec((tm, HG_W), row),
                  pl.BlockSpec((D_MODEL, D_MODEL), c2, pipeline_mode=once),
                  pl.BlockSpec((1, D_MODEL), c2),
                  pl.BlockSpec((D_MODEL, 2 * V7X_LANES), c2, pipeline_mode=once),
                  pl.BlockSpec((1, V7X_LANES), c2),
                  pl.BlockSpec((N_EXPERTS, D_MODEL, D_EXPERT), c3, pipeline_mode=once),
                  pl.BlockSpec((N_EXPERTS, D_MODEL, D_EXPERT), c3, pipeline_mode=once),
                  pl.BlockSpec((N_EXPERTS * D_EXPERT, D_MODEL), c2, pipeline_mode=once),
                  pl.BlockSpec((1, D_MODEL), c2)],
        out_specs=pl.BlockSpec((tm, D_MODEL), row),
        out_shape=jax.ShapeDtypeStruct((t, D_MODEL), F32),
        scratch_shapes=[pltpu.VMEM((tm, D_MODEL), BF16),
                        pltpu.VMEM((tm, V7X_LANES), F32),
                        pltpu.VMEM((tm, D_MODEL), F32)],
        compiler_params=_cparams("arbitrary"),
        name="mix_moe",
    )(x, oda, ohg, wo_b, norm2_w, wr, br, w1_b, w3_b, w2_b, final_w)


def _prompt_distances():
    tb = ATT_BLOCK
    key = jnp.arange(tb, dtype=jnp.int32)[:, None]
    qry = jnp.arange(tb, dtype=jnp.int32)[None, :]
    return jnp.concatenate([tb + qry - key, qry - key, qry - key - tb], axis=0)


def _sample_distances(dec_seq):
    t = (jnp.arange(8, dtype=jnp.int32) % 4)[:, None]
    key = (jnp.arange(PAGE_SIZE * DA_HEADS, dtype=jnp.int32) // DA_HEADS)[None, :]
    far = jnp.full((8, PAGE_SIZE * DA_HEADS), 1 << 20, jnp.int32)
    near = PAGE_SIZE + t - key
    new = jnp.where(key < dec_seq, t - key, -1)
    return jnp.concatenate([far, near, new], axis=0)


def kernel(x_prompt, x_sample, cache_k, cache_v, state_hgrn, page_table, norm1_w, w_in, rel_bias,
           lambda_q1, lambda_k1, lambda_q2, lambda_k2, da_subln_w, hg_lb_logits, hg_norm_w, w_out,
           norm2_w, router_g_w, router_g_b, router_e_w, router_e_b, expert_w1, expert_w3,
           expert_w2, final_norm_w):
    batch, seq, _ = x_prompt.shape
    nb, dec_seq, _ = x_sample.shape
    assert dec_seq == 4 and seq % ATT_BLOCK == 0 and seq % HG_CHUNK == 0
    l = 0

    lam = (jnp.exp(jnp.sum(lambda_q1[l].astype(F32) * lambda_k1[l].astype(F32)))
           - jnp.exp(jnp.sum(lambda_q2[l].astype(F32) * lambda_k2[l].astype(F32))) + LAM_INIT)
    lam = lam.reshape(1)
    lb = jnp.cumsum(jax.nn.softmax(hg_lb_logits.astype(F32), axis=0), axis=0)[l].reshape(1, HG_W)
    w_in_b = w_in[l].astype(BF16)
    wo_b = w_out[l].astype(BF16)
    w1_b = expert_w1[l].astype(BF16)
    w3_b = expert_w3[l].astype(BF16)
    w2_b = expert_w2[l].astype(BF16).reshape(N_EXPERTS * D_EXPERT, D_MODEL)
    pad = V7X_LANES - N_GROUPS - N_EXPERTS
    wr_f = jnp.pad(jnp.concatenate([router_g_w[l], router_e_w[l]], axis=1), ((0, 0), (0, pad)))
    wr_hi = wr_f.astype(BF16)
    wr = jnp.concatenate([wr_hi, (wr_f - wr_hi.astype(F32)).astype(BF16)], axis=1)
    br = jnp.pad(jnp.concatenate([router_g_b[l], router_e_b[l]]), (0, pad)).reshape(1, V7X_LANES)
    n1 = norm1_w[l].reshape(1, D_MODEL)
    n2 = norm2_w[l].reshape(1, D_MODEL)
    fw = final_norm_w.reshape(1, D_MODEL)
    subln = da_subln_w[l].reshape(1, DA_V)
    hgn = hg_norm_w[l].reshape(1, HG_V)

    tp = batch * seq
    qb, k_p, v_p, kb, vb, hg = _proj(x_prompt.reshape(tp, D_MODEL), n1, w_in_b, 512)
    bias_p = _bias_tiles(rel_bias, _prompt_distances(), False)
    oda = _attn_prompt(lam, rel_bias, qb, kb, vb, bias_p, subln.reshape(DA_V, 1), batch, seq)
    s0p = jnp.zeros((batch, HG_HEADS, HG_K, HG_V), F32)
    ohg, s_p = _hgrn(hg, lb, hgn, s0p, batch, seq, HG_CHUNK, HG_SUBCHUNKS, HG_DIAG, HG_CHUNK)
    y_p = _mix_moe(x_prompt.reshape(tp, D_MODEL), oda, ohg, wo_b, n2, wr, br,
                   w1_b, w3_b, w2_b, fw, 512)

    ts = nb * dec_seq
    qs, k_s, v_s, ksb, vsb, hgs = _proj(x_sample.reshape(ts, D_MODEL), n1, w_in_b, ts)
    page_rows = PAGE_SIZE * DA_HEADS
    bias_s = _bias_tiles(rel_bias, _sample_distances(dec_seq), True)
    bias_s = bias_s.reshape(DA_HEADS, 3, 8, page_rows).transpose(1, 0, 2, 3).reshape(
        3, DA_HEADS * 8, page_rows)
    q_rows = jnp.broadcast_to(
        qs.reshape(nb, dec_seq, DA_HEADS, 1, DA_V).transpose(0, 2, 3, 1, 4),
        (nb, DA_HEADS, 2, dec_seq, DA_V)).reshape(nb, DA_HEADS * 8, DA_V)
    new_rows = lambda a: jnp.pad(a.reshape(nb, dec_seq * DA_HEADS, DA_V),
                                 ((0, 0), (0, (8 - dec_seq) * DA_HEADS), (0, 0)))
    n_pool = cache_k.shape[1]
    oda_s = _attn_sample(page_table, lam, q_rows, bias_s, new_rows(ksb), new_rows(vsb), subln,
                         cache_k[l].reshape(n_pool, page_rows, DA_V),
                         cache_v[l].reshape(n_pool, page_rows, DA_V))
    hgs_pad = jnp.pad(hgs.reshape(nb, dec_seq, 4 * HG_W),
                      ((0, 0), (0, SAMPLE_CHUNK - dec_seq), (0, 0)))
    ohg_s, s_s = _hgrn(hgs_pad.reshape(nb * SAMPLE_CHUNK, 4 * HG_W), lb, hgn, state_hgrn[l],
                       nb, SAMPLE_CHUNK, SAMPLE_CHUNK, 1, SAMPLE_CHUNK, dec_seq)
    ohg_s = ohg_s.reshape(nb, SAMPLE_CHUNK, HG_W)[:, :dec_seq].reshape(ts, HG_W)
    y_s = _mix_moe(x_sample.reshape(ts, D_MODEL), oda_s.reshape(ts, DA_W), ohg_s, wo_b, n2, wr, br,
                   w1_b, w3_b, w2_b, fw, ts)

    return (y_p.reshape(batch, seq, D_MODEL),
            y_s.reshape(nb, dec_seq, D_MODEL),
            k_p.reshape(1, batch, seq, DA_HEADS, 2 * DA_QK),
            v_p.reshape(1, batch, seq, DA_HEADS, DA_V),
            s_p.reshape(1, batch, HG_HEADS, HG_K, HG_V),
            k_s.reshape(1, nb, dec_seq, DA_HEADS, 2 * DA_QK),
            v_s.reshape(1, nb, dec_seq, DA_HEADS, DA_V),
            s_s.reshape(1, nb, HG_HEADS, HG_K, HG_V))
```

```python
import functools
import math

import jax
import jax.numpy as jnp
from jax import lax
from jax.experimental import pallas as pl
from jax.experimental.pallas import tpu as pltpu

F32 = jnp.float32
BF16 = jnp.bfloat16

D_MODEL = 1024
PAGE_SIZE = 128
DA_HEADS = 4
DA_QK = 64
DA_V = 128
DA_W = DA_HEADS * DA_V
HG_HEADS = 4
HG_K = 128
HG_V = 128
HG_W = HG_HEADS * HG_V
N_BUCKETS = 32
MAX_DISTANCE = 128
N_GROUPS = 4
EXPERTS_PER_GROUP = 4
N_EXPERTS = N_GROUPS * EXPERTS_PER_GROUP
D_EXPERT = 256
EPS = 1e-6
IN_TOTAL = 7 * 512
LAM_INIT = 0.8 - 0.6 * math.exp(-0.3 * 0)

NEG_BIG = -1e30
V7X_LANES = 128
VMEM_LIMIT = 56 * 1024 * 1024

def _t5_bucket_py(n):
    max_exact = N_BUCKETS // 2
    if n < max_exact:
        return n
    return min(max_exact + int(math.log(n / max_exact) / math.log(MAX_DISTANCE / max_exact)
                               * (N_BUCKETS - max_exact)), N_BUCKETS - 1)


LAST_BUCKET_START = next(n for n in range(MAX_DISTANCE + 1) if _t5_bucket_py(n) == N_BUCKETS - 1)

ATT_BLOCK = 512
SUM_ROWS = 16
HG_CHUNK = 128
HG_SUBCHUNKS = 2
HG_DIAG = 4
SAMPLE_CHUNK = 16
MOE_SLAB = 128
PAGES_PER_STEP = 32
SAMPLE_STREAMS = 1


def _cparams(*sem):
    return pltpu.CompilerParams(dimension_semantics=sem, vmem_limit_bytes=VMEM_LIMIT)


def _dot(a, b):
    return jnp.dot(a, b, preferred_element_type=F32)


def _dot_nt(a, b):
    return lax.dot_general(a, b, (((1,), (1,)), ((), ())), preferred_element_type=F32)


def _dot_tn(a, b):
    return lax.dot_general(a, b, (((0,), (0,)), ((), ())), preferred_element_type=F32)


def _rms(x, w):
    return x * lax.rsqrt(jnp.mean(x * x, axis=-1, keepdims=True) + EPS) * w


def _bias_body(rb_ref, n_ref, o_ref, *, interleaved):
    h = pl.program_id(0)
    n = n_ref[...]
    if interleaved:
        lane = lax.broadcasted_iota(jnp.int32, n.shape, 1)
        n = jnp.where(lane % DA_HEADS == h, n, -1)
    nn = jnp.maximum(n, 0)
    max_exact = N_BUCKETS // 2
    nf = jnp.maximum(nn, max_exact).astype(F32)
    large = max_exact + jnp.floor(jnp.log(nf / max_exact) / math.log(MAX_DISTANCE / max_exact)
                                  * (N_BUCKETS - max_exact)).astype(jnp.int32)
    large = jnp.minimum(large, N_BUCKETS - 1)
    bucket = jnp.where(nn < max_exact, nn, large)
    bias = jnp.zeros(n.shape, F32)
    for b in range(N_BUCKETS):
        bias = jnp.where(bucket == b, rb_ref[b, h], bias)
    o_ref[0] = jnp.where(n >= 0, bias, NEG_BIG)


def _bias_tiles(rel_bias, n, interleaved):
    r, c = n.shape
    return pl.pallas_call(
        functools.partial(_bias_body, interleaved=interleaved),
        grid=(DA_HEADS,),
        in_specs=[pl.BlockSpec(memory_space=pltpu.SMEM),
                  pl.BlockSpec((r, c), lambda h: (0, 0))],
        out_specs=pl.BlockSpec((1, r, c), lambda h: (h, 0, 0)),
        out_shape=jax.ShapeDtypeStruct((DA_HEADS, r, c), F32),
        compiler_params=_cparams("arbitrary"),
        name="bias_tiles",
    )(rel_bias, n)


def _proj_body(x_ref, nw_ref, w_ref, q_ref, k_ref, v_ref, kb_ref, vb_ref, hg_ref):
    hb = _rms(x_ref[...], nw_ref[...]).astype(BF16)
    q_ref[...] = (_dot(hb, w_ref[:, 0:512]) * (DA_QK ** -0.5)).astype(BF16)
    tm = x_ref.shape[0]
    k = _dot(hb, w_ref[:, 512:1024])
    kb_ref[...] = k.astype(BF16)
    v = _dot(hb, w_ref[:, 1024:1536])
    vb_ref[...] = v.astype(BF16)
    for h in range(DA_HEADS):
        k_ref[pl.ds(h, tm, stride=DA_HEADS), :] = k[:, h * DA_V:(h + 1) * DA_V]
        v_ref[pl.ds(h, tm, stride=DA_HEADS), :] = v[:, h * DA_V:(h + 1) * DA_V]
    hg_ref[...] = _dot(hb, w_ref[:, 1536:IN_TOTAL])


def _proj(x, norm_w, w_in_b, tm):
    t = x.shape[0]
    row = lambda i: (i, 0)
    const = lambda i: (0, 0)
    return pl.pallas_call(
        _proj_body,
        grid=(t // tm,),
        in_specs=[pl.BlockSpec((tm, D_MODEL), row),
                  pl.BlockSpec((1, D_MODEL), const),
                  pl.BlockSpec((D_MODEL, IN_TOTAL), const)],
        out_specs=[pl.BlockSpec((tm, 512), row),
                   pl.BlockSpec((tm * DA_HEADS, DA_V), row),
                   pl.BlockSpec((tm * DA_HEADS, DA_V), row),
                   pl.BlockSpec((tm, 512), row),
                   pl.BlockSpec((tm, 512), row),
                   pl.BlockSpec((tm, 2048), row)],
        out_shape=[jax.ShapeDtypeStruct((t, 512), BF16),
                   jax.ShapeDtypeStruct((t * DA_HEADS, DA_V), F32),
                   jax.ShapeDtypeStruct((t * DA_HEADS, DA_V), F32),
                   jax.ShapeDtypeStruct((t, 512), BF16),
                   jax.ShapeDtypeStruct((t, 512), BF16),
                   jax.ShapeDtypeStruct((t, 2048), F32)],
        compiler_params=_cparams("arbitrary"),
        name="proj",
    )(x, norm_w, w_in_b)


def _softmax_update(s, v, m_ref, l_ref, acc_ref, idx):
    m_prev = m_ref[idx]
    m_new = jnp.maximum(m_prev, jnp.max(s, axis=-1, keepdims=True))
    alpha = jnp.exp(m_prev - m_new)
    p = jnp.exp(s - m_new)
    l_ref[idx] = alpha * l_ref[idx] + jnp.sum(p, axis=-1, keepdims=True)
    acc_ref[idx] = alpha * acc_ref[idx] + _dot(p.astype(BF16), v)
    m_ref[idx] = m_new


def _sub_ln(o, w):
    return _rms(o, w) * (1.0 - LAM_INIT)


def _attn_prompt_body(lam_ref, rb_ref, q_ref, k_ref, v_ref, bias_ref, w_ref, o_ref,
                      vt_ref, m_ref, acc_ref, s_ref):
    h = pl.program_id(1)
    qi = pl.program_id(2)
    tb = ATT_BLOCK
    nblk = vt_ref.shape[0]

    @pl.when(qi == 0)
    def _():
        for j in range(nblk):
            vt_ref[j, 0:DA_V, :] = v_ref[j * tb:(j + 1) * tb, :].T
            vt_ref[j, DA_V:DA_V + SUM_ROWS, :] = jnp.ones((SUM_ROWS, tb), BF16)

    q = q_ref[...]
    lane = lax.broadcasted_iota(jnp.int32, q.shape, 1)
    zero = jnp.zeros_like(q)
    q2 = jnp.concatenate([jnp.where(lane < DA_QK, q, zero), jnp.where(lane >= DA_QK, q, zero)],
                         axis=0)

    m_ref[...] = jnp.full(m_ref.shape, NEG_BIG, F32)
    acc_ref[...] = jnp.zeros(acc_ref.shape, F32)

    def scores(first_blk, nblocks):
        start = first_blk * tb
        if not isinstance(start, int):
            start = pl.multiple_of(start, tb)
        return _dot_nt(k_ref[pl.ds(start, nblocks * tb), :], q2)

    def fold(s, first_blk, nblocks, bias, far_bias):
        m_prev = m_ref[...]
        if far_bias is None:
            s = s + jnp.concatenate([bias, bias], axis=1)
            m_new = jnp.maximum(m_prev, jnp.max(s, axis=0, keepdims=True))
            shift = m_new
        else:
            m_new = jnp.maximum(m_prev, jnp.max(s, axis=0, keepdims=True) + far_bias)
            shift = m_new - far_bias
        p = jnp.exp(s - shift).astype(BF16)
        pv = _dot(vt_ref[first_blk], p[0:tb])
        for i in range(1, nblocks):
            pv = pv + _dot(vt_ref[first_blk + i], p[i * tb:(i + 1) * tb])
        acc_ref[...] = jnp.exp(m_prev - m_new) * acc_ref[...] + pv
        m_ref[...] = m_new

    far_bias = rb_ref[N_BUCKETS - 1, h]
    n_far = jnp.maximum(qi - 1, 0)
    odd = n_far % 2


    last = n_far // 2
    pair_blk = lambda p: odd + 2 * p

    def near_fold(slot):
        off = pl.multiple_of(jnp.where(qi == 0, tb, 0), tb)
        fold(s_ref[slot], pair_blk(last), 2, bias_ref[0, pl.ds(off, 2 * tb), :], None)

    @pl.when(odd == 0)
    def _():
        s_ref[0] = scores(pair_blk(0), 2)

    @pl.when(odd == 1)
    def _():
        s_single = scores(0, 1)
        s_ref[0] = scores(1, 2)
        fold(s_single, 0, 1, None, far_bias)

    def two_far_pairs(t, carry):
        s_ref[1] = scores(pair_blk(2 * t + 1), 2)
        fold(s_ref[0], pair_blk(2 * t), 2, None, far_bias)
        s_ref[0] = scores(pair_blk(2 * t + 2), 2)
        fold(s_ref[1], pair_blk(2 * t + 1), 2, None, far_bias)
        return carry

    lax.fori_loop(0, last // 2, two_far_pairs, 0)

    @pl.when(last % 2 == 0)
    def _():
        near_fold(0)

    @pl.when(last % 2 == 1)
    def _():
        s_ref[1] = scores(pair_blk(last), 2)
        fold(s_ref[0], pair_blk(last - 1), 2, None, far_bias)
        near_fold(1)

    a0 = acc_ref[:, 0:tb]
    a1 = acc_ref[:, tb:2 * tb]
    o = (a0[0:DA_V] / a0[DA_V:DA_V + 1]
         - lam_ref[0] * (a1[0:DA_V] / a1[DA_V:DA_V + 1]))
    o = o * lax.rsqrt(jnp.mean(o * o, axis=0, keepdims=True) + EPS) * w_ref[...]
    o_ref[...] = (o * (1.0 - LAM_INIT)).T.astype(BF16)


def _attn_prompt(lam, rel_bias, qb, kb, vb, bias, subln_col, batch, seq):
    tb = ATT_BLOCK
    assert tb + 1 >= LAST_BUCKET_START, "far blocks must lie entirely in the last T5 bucket"
    nq = seq // tb
    return pl.pallas_call(
        _attn_prompt_body,
        grid=(batch, DA_HEADS, nq),
        in_specs=[pl.BlockSpec(memory_space=pltpu.SMEM),
                  pl.BlockSpec(memory_space=pltpu.SMEM),
                  pl.BlockSpec((tb, DA_V), lambda b, h, i: (b * nq + i, h)),
                  pl.BlockSpec((seq, DA_V), lambda b, h, i: (b, h)),
                  pl.BlockSpec((seq, DA_V), lambda b, h, i: (b, h)),
                  pl.BlockSpec((1, 3 * tb, tb), lambda b, h, i: (h, 0, 0)),
                  pl.BlockSpec((DA_V, 1), lambda b, h, i: (0, 0))],
        out_specs=pl.BlockSpec((tb, DA_V), lambda b, h, i: (b * nq + i, h)),
        out_shape=jax.ShapeDtypeStruct((batch * seq, DA_W), BF16),
        scratch_shapes=[pltpu.VMEM((nq, DA_V + SUM_ROWS, tb), BF16),
                        pltpu.VMEM((1, 2 * tb), F32),
                        pltpu.VMEM((DA_V + SUM_ROWS, 2 * tb), F32),
                        pltpu.VMEM((2, 2 * tb, 2 * tb), F32)],
        compiler_params=_cparams("arbitrary", "arbitrary", "arbitrary"),
        name="attn_prompt",
    )(lam, rel_bias, qb, kb, vb, bias, subln_col)


def _attn_sample_body(pt_ref, lam_ref, q_ref, bias_ref, kn_ref, vn_ref, w_ref, *rest):
    npg = PAGES_PER_STEP
    k_refs = rest[:npg]
    v_refs = rest[npg:2 * npg]
    o_ref = rest[2 * npg]
    m_ref, l_ref, acc_ref = rest[2 * npg + 1:]
    j = pl.program_id(1)
    last = pl.num_programs(1) - 1
    rows = DA_HEADS * 8

    @pl.when(j == 0)
    def _():
        m_ref[...] = jnp.full(m_ref.shape, NEG_BIG, F32)
        l_ref[...] = jnp.zeros(l_ref.shape, F32)
        acc_ref[...] = jnp.zeros(acc_ref.shape, F32)

    q = q_ref[0]
    lane = lax.broadcasted_iota(jnp.int32, q.shape, 1)
    row = lax.broadcasted_iota(jnp.int32, q.shape, 0)
    w = jnp.where(lane // DA_QK == (row // 4) % 2, q, jnp.zeros_like(q))

    far = bias_ref[0]
    near = bias_ref[1]
    per = npg // SAMPLE_STREAMS
    for g in range(SAMPLE_STREAMS):
        pages = range(g * per, (g + 1) * per)
        kcat = jnp.concatenate([k_refs[i][0].astype(BF16) for i in pages], axis=0)
        bias = jnp.concatenate(
            [far if i < npg - 1 else jnp.where(j == last, near, far) for i in pages], axis=1)
        s = _dot_nt(w, kcat) + bias
        vcat = jnp.concatenate([v_refs[i][0].astype(BF16) for i in pages], axis=0)
        _softmax_update(s, vcat, m_ref, l_ref, acc_ref, g)

    @pl.when(j == last)
    def _():
        s_new = _dot_nt(w, kn_ref[0]) + bias_ref[2][:, 0:rows]
        _softmax_update(s_new, vn_ref[0], m_ref, l_ref, acc_ref, 0)
        m_all = m_ref[0]
        for g in range(1, SAMPLE_STREAMS):
            m_all = jnp.maximum(m_all, m_ref[g])
        l_all = jnp.zeros_like(m_all)
        acc_all = jnp.zeros(acc_ref.shape[1:], F32)
        for g in range(SAMPLE_STREAMS):
            scale = jnp.exp(m_ref[g] - m_all)
            l_all = l_all + scale * l_ref[g]
            acc_all = acc_all + scale * acc_ref[g]
        normed = acc_all / l_all
        for h in range(DA_HEADS):
            blk = normed[h * 8:(h + 1) * 8, :]
            o = blk[0:4] - lam_ref[0] * blk[4:8]
            o_ref[0, :, h * DA_V:(h + 1) * DA_V] = _sub_ln(o, w_ref[...]).astype(BF16)


def _attn_sample(page_table, lam, q_rows, bias, k_new, v_new, subln_w, cache_k, cache_v):
    nb, n_pages = page_table.shape
    npg = PAGES_PER_STEP
    assert n_pages % npg == 0, (n_pages, npg)
    rows = DA_HEADS * 8
    page_rows = PAGE_SIZE * DA_HEADS

    def page_spec(i):
        return pl.BlockSpec((1, page_rows, DA_V),
                            lambda b, j, pt: (pt[b, j * npg + i], 0, 0))

    seq_spec = lambda shape: pl.BlockSpec(shape, lambda b, j, pt: (b, 0, 0))
    grid_spec = pltpu.PrefetchScalarGridSpec(
        num_scalar_prefetch=1,
        grid=(nb, n_pages // npg),
        in_specs=[pl.BlockSpec(memory_space=pltpu.SMEM),
                  seq_spec((1, rows, DA_V)),
                  pl.BlockSpec((3, rows, page_rows), lambda b, j, pt: (0, 0, 0)),
                  seq_spec((1, rows, DA_V)),
                  seq_spec((1, rows, DA_V)),
                  pl.BlockSpec((1, DA_V), lambda b, j, pt: (0, 0))]
                 + [page_spec(i) for i in range(npg)] * 2,
        out_specs=seq_spec((1, 4, DA_W)),
        scratch_shapes=[pltpu.VMEM((SAMPLE_STREAMS, rows, 1), F32),
                        pltpu.VMEM((SAMPLE_STREAMS, rows, 1), F32),
                        pltpu.VMEM((SAMPLE_STREAMS, rows, DA_V), F32)],
    )
    return pl.pallas_call(
        _attn_sample_body,
        grid_spec=grid_spec,
        out_shape=jax.ShapeDtypeStruct((nb, 4, DA_W), BF16),
        compiler_params=_cparams("arbitrary", "arbitrary"),
        name="attn_sample",
    )(page_table, lam, q_rows, bias, k_new, v_new, subln_w,
      *([cache_k] * npg), *([cache_v] * npg))


def _split3(x):
    hi = x.astype(BF16)
    r = x - hi.astype(F32)
    mid = r.astype(BF16)
    lo = (r - mid.astype(F32)).astype(BF16)
    return jnp.concatenate([hi, mid, lo], axis=1)


def _merge3(p):
    w = p.shape[1] // 3
    return p[:, 0:w] + p[:, w:2 * w] + p[:, 2 * w:3 * w]


def _hgrn_chunks(hg_ref, lb_ref, nw_ref, st_ref, o_ref, chunk, nsub, diag, valid):
    c = chunk
    units = [(s, h) for s in range(nsub) for h in range(HG_HEADS)]
    part = lambda p, u: hg_ref[u[0] * c:(u[0] + 1) * c,
                               p * HG_W + u[1] * HG_K: p * HG_W + (u[1] + 1) * HG_K]
    row = lax.broadcasted_iota(jnp.int32, (c, HG_K), 0)
    r2 = lax.broadcasted_iota(jnp.int32, (c, c), 0)
    c2 = lax.broadcasted_iota(jnp.int32, (c, c), 1)
    tri = jnp.where(c2 <= r2, 1.0, 0.0).astype(BF16)

    q, kk, vb, gc = {}, {}, {}, {}
    for u in units:
        hq = part(0, u)
        lb = lb_ref[:, u[1] * HG_K:(u[1] + 1) * HG_K]
        qu = hq * jax.nn.sigmoid(hq) * (HG_K ** -0.5)
        f = lb + (1.0 - lb) * jax.nn.sigmoid(part(1, u))
        if valid < c:
            f = jnp.where(row < valid, f, 1.0)
            qu = jnp.where(row < valid, qu, 0.0)
        q[u] = qu
        kk[u] = 1.0 - f
        vb[u] = part(2, u).astype(BF16)
        gc[u] = _merge3(_dot(tri, _split3(jnp.log2(f))))

    packed = c == V7X_LANES
    a = {u: jnp.zeros((c, c), F32) for u in units}
    rmod = row % diag
    for d in range(diag):
        for u in units:
            if d == 0:
                x = q[u] * kk[u]
            else:
                e = jnp.where(rmod >= d, gc[u] - pltpu.roll(gc[u], d, axis=0), NEG_BIG)
                x = q[u] * pltpu.roll(kk[u], d, axis=0) * jnp.exp2(e)
            xsum = jnp.sum(x, axis=-1, keepdims=True)
            if packed:
                a[u] = jnp.where(c2 == (c - d) % c, xsum, a[u])
            else:
                a[u] = a[u] + jnp.where(r2 - c2 == d, xsum, 0.0)
    if packed:
        for u in units:
            a[u] = pltpu.roll(a[u], 0, axis=1, stride=1, stride_axis=0)

    levels = []
    half = diag
    while half < c:
        levels.append(half)
        half *= 2
    def boundary(g, half):
        reps = [jnp.broadcast_to(g[b + half - 1:b + half, :], (2 * half, HG_K))
                for b in range(0, c, 2 * half)]
        return reps[0] if len(reps) == 1 else jnp.concatenate(reps, axis=0)

    if levels:
        for half in levels:
            second = (row % (2 * half)) >= half
            same = r2 // (2 * half) == c2 // (2 * half)
            for u in units:
                bnd = boundary(gc[u], half)
                qh = q[u] * jnp.exp2(jnp.where(second, gc[u] - bnd, NEG_BIG))
                kh = kk[u] * jnp.exp2(jnp.where(second, NEG_BIG, bnd - gc[u]))
                al = _dot_nt(qh.astype(BF16), kh.astype(BF16))
                a[u] = a[u] + jnp.where(same, al, 0.0)

    for u in units:
        s, h = u
        st = st_ref[h]
        g_last = gc[u][c - 1:c, :]
        o = (_dot_nt((q[u] * jnp.exp2(gc[u])).astype(BF16), st.astype(BF16))
             + _dot(a[u].astype(BF16), vb[u]))
        kl = (kk[u] * jnp.exp2(g_last - gc[u])).astype(BF16)
        st_ref[h] = st * jnp.exp2(g_last) + _dot_tn(vb[u], kl)
        hgate = part(3, u)
        o = _rms(o, nw_ref[...]) * (hgate * jax.nn.sigmoid(hgate))
        o_ref[s * c:(s + 1) * c, h * HG_V:(h + 1) * HG_V] = o.astype(BF16)


def _hgrn_body(hg_ref, lb_ref, nw_ref, s0_ref, o_ref, sf_ref, st_ref, *, chunk, nsub, diag, valid):
    ci = pl.program_id(1)

    @pl.when(ci == 0)
    def _():
        for h in range(HG_HEADS):
            st_ref[h] = s0_ref[0, h].T

    _hgrn_chunks(hg_ref, lb_ref, nw_ref, st_ref, o_ref, chunk, nsub, diag, valid)

    @pl.when(ci == pl.num_programs(1) - 1)
    def _():
        for h in range(HG_HEADS):
            sf_ref[0, h] = st_ref[h].T


def _hgrn(hg, lb, norm_w, s0, batch, length, chunk, nsub, diag, valid):
    rows = chunk * nsub
    nc = length // rows
    body = functools.partial(_hgrn_body, chunk=chunk, nsub=nsub, diag=diag, valid=valid)
    state_spec = pl.BlockSpec((1, HG_HEADS, HG_K, HG_V), lambda b, c: (b, 0, 0, 0))
    return pl.pallas_call(
        body,
        grid=(batch, nc),
        in_specs=[pl.BlockSpec((rows, 4 * HG_W), lambda b, c: (b * nc + c, 0)),
                  pl.BlockSpec((1, HG_W), lambda b, c: (0, 0)),
                  pl.BlockSpec((1, HG_V), lambda b, c: (0, 0)),
                  state_spec],
        out_specs=[pl.BlockSpec((rows, HG_W), lambda b, c: (b * nc + c, 0)), state_spec],
        out_shape=[jax.ShapeDtypeStruct((batch * length, HG_W), BF16),
                   jax.ShapeDtypeStruct((batch, HG_HEADS, HG_K, HG_V), F32)],
        scratch_shapes=[pltpu.VMEM((HG_HEADS, HG_V, HG_K), F32)],
        compiler_params=_cparams("arbitrary", "arbitrary"),
        name="hgrn",
    )(hg, lb, norm_w, s0)


def _route(logits):
    lane = lax.broadcasted_iota(jnp.int32, logits.shape, 1)
    neg = jnp.float32(-jnp.inf)

    def top(mask):
        masked = jnp.where(mask, logits, neg)
        idx = jnp.argmax(masked, axis=-1, keepdims=True).astype(jnp.int32)
        return jnp.max(masked, axis=-1, keepdims=True), idx

    gmask = lane < N_GROUPS
    gmax, gidx = top(gmask)
    p_top = 1.0 / jnp.sum(jnp.where(gmask, jnp.exp(logits - gmax), 0.0), axis=-1, keepdims=True)
    lo = N_GROUPS + gidx * EXPERTS_PER_GROUP
    emask = (lane >= lo) & (lane < lo + EXPERTS_PER_GROUP)
    t1, i1 = top(emask)
    t2, i2 = top(emask & (lane != i1))
    e2 = jnp.exp(t2 - t1)
    g1 = p_top / (1.0 + e2)
    g2 = p_top * e2 / (1.0 + e2)
    return gidx, i1 - N_GROUPS, i2 - N_GROUPS, g1, g2


def _mix_moe_body(x_ref, oda_ref, ohg_ref, wo_ref, n2_ref, wr_ref, br_ref,
                  w1_ref, w3_ref, w2_ref, fw_ref, y_ref, xs_ref, cws_ref, ys_ref):
    tm = x_ref.shape[0]
    x1 = (x_ref[...] + _dot(oda_ref[...], wo_ref[0:DA_W, :])
          + _dot(ohg_ref[...], wo_ref[DA_W:DA_W + HG_W, :]))
    h2 = _rms(x1, n2_ref[...])
    hb = h2.astype(BF16)
    hl = (h2 - hb.astype(F32)).astype(BF16)
    hw = _dot(hb, wr_ref[...])
    logits = (hw[:, 0:V7X_LANES] + hw[:, V7X_LANES:2 * V7X_LANES]
              + _dot(hl, wr_ref[:, 0:V7X_LANES])) + br_ref[...]
    gidx, id1, id2, g1, g2 = _route(logits)

    lane = lax.broadcasted_iota(jnp.int32, logits.shape, 1)
    base = gidx * EXPERTS_PER_GROUP
    cw = jnp.where(lane == id1 - base, g1, 0.0) + jnp.where(lane == id2 - base, g2, 0.0)

    gone = jnp.where(lane == gidx, 1.0, 0.0)
    r2 = lax.broadcasted_iota(jnp.int32, (tm, tm), 0)
    c2 = lax.broadcasted_iota(jnp.int32, (tm, tm), 1)
    before = _dot(jnp.where(c2 < r2, 1.0, 0.0).astype(BF16), gone.astype(BF16))
    sizes_row = before[tm - 1:tm] + gone[tm - 1:tm]
    lane1 = lax.broadcasted_iota(jnp.int32, (1, V7X_LANES), 1)
    sizes = [jnp.sum(jnp.where(lane1 == g, sizes_row, 0.0)).astype(jnp.int32)
             for g in range(N_GROUPS)]
    starts = [jnp.int32(0)]
    for g in range(1, N_GROUPS):
        starts.append(starts[-1] + sizes[g - 1])
    rank = jnp.sum(jnp.where(lane == gidx, before, 0.0), axis=-1, keepdims=True).astype(jnp.int32)
    pos = rank
    for g in range(1, N_GROUPS):
        pos = pos + jnp.where(gidx == g, starts[g], 0)
    perm = jnp.where(c2 == pos, 1.0, 0.0).astype(BF16)

    xs_ref[...] = _dot_tn(perm, hb).astype(BF16)
    cw_hi = cw.astype(BF16)
    cw_lo = (cw - cw_hi.astype(F32)).astype(BF16)
    cws = _dot_tn(perm, jnp.concatenate([cw_hi, cw_lo], axis=1))
    cws_ref[...] = cws[:, 0:V7X_LANES] + cws[:, V7X_LANES:2 * V7X_LANES]
    ys_ref[...] = jnp.zeros(ys_ref.shape, F32)

    rowi = lax.broadcasted_iota(jnp.int32, (MOE_SLAB, 1), 0)
    group_w = EXPERTS_PER_GROUP * D_EXPERT
    for g in range(N_GROUPS):
        lo = starts[g]
        hi = starts[g] + sizes[g]

        def slab(sb, carry, g=g, lo=lo, hi=hi):
            r0 = pl.multiple_of(sb * MOE_SLAB, MOE_SLAB)
            rows = xs_ref[pl.ds(r0, MOE_SLAB), :]
            mine = (r0 + rowi >= lo) & (r0 + rowi < hi)
            cwb = cws_ref[pl.ds(r0, MOE_SLAB), :]
            parts = []
            for j in range(EXPERTS_PER_GROUP):
                e = g * EXPERTS_PER_GROUP + j
                he = _dot(rows, w1_ref[e])
                he = he * jax.nn.sigmoid(he) * _dot(rows, w3_ref[e])
                parts.append((jnp.where(mine, cwb[:, j:j + 1], 0.0) * he).astype(BF16))
            ys_ref[pl.ds(r0, MOE_SLAB), :] += _dot(jnp.concatenate(parts, axis=1),
                                                   w2_ref[g * group_w:(g + 1) * group_w, :])
            return carry

        lax.fori_loop(lo // MOE_SLAB, (hi + MOE_SLAB - 1) // MOE_SLAB, slab, 0)

    y = _dot(perm, ys_ref[...].astype(BF16))
    y_ref[...] = _rms(x1 + y, fw_ref[...])


def _mix_moe(x, oda, ohg, wo_b, norm2_w, wr, br, w1_b, w3_b, w2_b, final_w, tm):
    t = x.shape[0]
    row = lambda i: (i, 0)
    c2 = lambda i: (0, 0)
    c3 = lambda i: (0, 0, 0)
    once = pl.Buffered(1)
    return pl.pallas_call(
        _mix_moe_body,
        grid=(t // tm,),
        in_specs=[pl.BlockSpec((tm, D_MODEL), row),
                  pl.BlockSpec((tm, DA_W), row),
                  pl.BlockSpec((tm, HG_W), row),
                  pl.BlockSpec((D_MODEL, D_MODEL), c2, pipeline_mode=once),
                  pl.BlockSpec((1, D_MODEL), c2),
                  pl.BlockSpec((D_MODEL, 2 * V7X_LANES), c2, pipeline_mode=once),
                  pl.BlockSpec((1, V7X_LANES), c2),
                  pl.BlockSpec((N_EXPERTS, D_MODEL, D_EXPERT), c3, pipeline_mode=once),
                  pl.BlockSpec((N_EXPERTS, D_MODEL, D_EXPERT), c3, pipeline_mode=once),
                  pl.BlockSpec((N_EXPERTS * D_EXPERT, D_MODEL), c2, pipeline_mode=once),
                  pl.BlockSpec((1, D_MODEL), c2)],
        out_specs=pl.BlockSpec((tm, D_MODEL), row),
        out_shape=jax.ShapeDtypeStruct((t, D_MODEL), F32),
        scratch_shapes=[pltpu.VMEM((tm, D_MODEL), BF16),
                        pltpu.VMEM((tm, V7X_LANES), F32),
                        pltpu.VMEM((tm, D_MODEL), F32)],
        compiler_params=_cparams("arbitrary"),
        name="mix_moe",
    )(x, oda, ohg, wo_b, norm2_w, wr, br, w1_b, w3_b, w2_b, final_w)


def _prompt_distances():
    tb = ATT_BLOCK
    key = jnp.arange(tb, dtype=jnp.int32)[:, None]
    qry = jnp.arange(tb, dtype=jnp.int32)[None, :]
    return jnp.concatenate([tb + qry - key, qry - key, qry - key - tb], axis=0)


def _sample_distances(dec_seq):
    t = (jnp.arange(8, dtype=jnp.int32) % 4)[:, None]
    key = (jnp.arange(PAGE_SIZE * DA_HEADS, dtype=jnp.int32) // DA_HEADS)[None, :]
    far = jnp.full((8, PAGE_SIZE * DA_HEADS), 1 << 20, jnp.int32)
    near = PAGE_SIZE + t - key
    new = jnp.where(key < dec_seq, t - key, -1)
    return jnp.concatenate([far, near, new], axis=0)


def kernel(x_prompt, x_sample, cache_k, cache_v, state_hgrn, page_table, norm1_w, w_in, rel_bias,
           lambda_q1, lambda_k1, lambda_q2, lambda_k2, da_subln_w, hg_lb_logits, hg_norm_w, w_out,
           norm2_w, router_g_w, router_g_b, router_e_w, router_e_b, expert_w1, expert_w3,
           expert_w2, final_norm_w):
    batch, seq, _ = x_prompt.shape
    nb, dec_seq, _ = x_sample.shape
    assert dec_seq == 4 and seq % ATT_BLOCK == 0 and seq % HG_CHUNK == 0
    l = 0

    lam = (jnp.exp(jnp.sum(lambda_q1[l].astype(F32) * lambda_k1[l].astype(F32)))
           - jnp.exp(jnp.sum(lambda_q2[l].astype(F32) * lambda_k2[l].astype(F32))) + LAM_INIT)
    lam = lam.reshape(1)
    lb = jnp.cumsum(jax.nn.softmax(hg_lb_logits.astype(F32), axis=0), axis=0)[l].reshape(1, HG_W)
    w_in_b = w_in[l].astype(BF16)
    wo_b = w_out[l].astype(BF16)
    w1_b = expert_w1[l].astype(BF16)
    w3_b = expert_w3[l].astype(BF16)
    w2_b = expert_w2[l].astype(BF16).reshape(N_EXPERTS * D_EXPERT, D_MODEL)
    pad = V7X_LANES - N_GROUPS - N_EXPERTS
    wr_f = jnp.pad(jnp.concatenate([router_g_w[l], router_e_w[l]], axis=1), ((0, 0), (0, pad)))
    wr_hi = wr_f.astype(BF16)
    wr = jnp.concatenate([wr_hi, (wr_f - wr_hi.astype(F32)).astype(BF16)], axis=1)
    br = jnp.pad(jnp.concatenate([router_g_b[l], router_e_b[l]]), (0, pad)).reshape(1, V7X_LANES)
    n1 = norm1_w[l].reshape(1, D_MODEL)
    n2 = norm2_w[l].reshape(1, D_MODEL)
    fw = final_norm_w.reshape(1, D_MODEL)
    subln = da_subln_w[l].reshape(1, DA_V)
    hgn = hg_norm_w[l].reshape(1, HG_V)

    tp = batch * seq
    qb, k_p, v_p, kb, vb, hg = _proj(x_prompt.reshape(tp, D_MODEL), n1, w_in_b, 512)
    bias_p = _bias_tiles(rel_bias, _prompt_distances(), False)
    oda = _attn_prompt(lam, rel_bias, qb, kb, vb, bias_p, subln.reshape(DA_V, 1), batch, seq)
    s0p = jnp.zeros((batch, HG_HEADS, HG_K, HG_V), F32)
    ohg, s_p = _hgrn(hg, lb, hgn, s0p, batch, seq, HG_CHUNK, HG_SUBCHUNKS, HG_DIAG, HG_CHUNK)
    y_p = _mix_moe(x_prompt.reshape(tp, D_MODEL), oda, ohg, wo_b, n2, wr, br,
                   w1_b, w3_b, w2_b, fw, 512)

    ts = nb * dec_seq
    qs, k_s, v_s, ksb, vsb, hgs = _proj(x_sample.reshape(ts, D_MODEL), n1, w_in_b, ts)
    page_rows = PAGE_SIZE * DA_HEADS
    bias_s = _bias_tiles(rel_bias, _sample_distances(dec_seq), True)
    bias_s = bias_s.reshape(DA_HEADS, 3, 8, page_rows).transpose(1, 0, 2, 3).reshape(
        3, DA_HEADS * 8, page_rows)
    q_rows = jnp.broadcast_to(
        qs.reshape(nb, dec_seq, DA_HEADS, 1, DA_V).transpose(0, 2, 3, 1, 4),
        (nb, DA_HEADS, 2, dec_seq, DA_V)).reshape(nb, DA_HEADS * 8, DA_V)
    new_rows = lambda a: jnp.pad(a.reshape(nb, dec_seq * DA_HEADS, DA_V),
                                 ((0, 0), (0, (8 - dec_seq) * DA_HEADS), (0, 0)))
    n_pool = cache_k.shape[1]
    oda_s = _attn_sample(page_table, lam, q_rows, bias_s, new_rows(ksb), new_rows(vsb), subln,
                         cache_k[l].reshape(n_pool, page_rows, DA_V),
                         cache_v[l].reshape(n_pool, page_rows, DA_V))
    hgs_pad = jnp.pad(hgs.reshape(nb, dec_seq, 4 * HG_W),
                      ((0, 0), (0, SAMPLE_CHUNK - dec_seq), (0, 0)))
    ohg_s, s_s = _hgrn(hgs_pad.reshape(nb * SAMPLE_CHUNK, 4 * HG_W), lb, hgn, state_hgrn[l],
                       nb, SAMPLE_CHUNK, SAMPLE_CHUNK, 1, HG_DIAG, dec_seq)
    ohg_s = ohg_s.reshape(nb, SAMPLE_CHUNK, HG_W)[:, :dec_seq].reshape(ts, HG_W)
    y_s = _mix_moe(x_sample.reshape(ts, D_MODEL), oda_s.reshape(ts, DA_W), ohg_s, wo_b, n2, wr, br,
                   w1_b, w3_b, w2_b, fw, ts)

    return (y_p.reshape(batch, seq, D_MODEL),
            y_s.reshape(nb, dec_seq, D_MODEL),
            k_p.reshape(1, batch, seq, DA_HEADS, 2 * DA_QK),
            v_p.reshape(1, batch, seq, DA_HEADS, DA_V),
            s_p.reshape(1, batch, HG_HEADS, HG_K, HG_V),
            k_s.reshape(1, nb, dec_seq, DA_HEADS, 2 * DA_QK),
            v_s.reshape(1, nb, dec_seq, DA_HEADS, DA_V),
            s_s.reshape(1, nb, HG_HEADS, HG_K, HG_V))
```

```python
import functools
import math

import jax
import jax.numpy as jnp
from jax import lax
from jax.experimental import pallas as pl
from jax.experimental.pallas import tpu as pltpu

F32 = jnp.float32
BF16 = jnp.bfloat16

D_MODEL = 1024
PAGE_SIZE = 128
DA_HEADS = 4
DA_QK = 64
DA_V = 128
DA_W = DA_HEADS * DA_V
HG_HEADS = 4
HG_K = 128
HG_V = 128
HG_W = HG_HEADS * HG_V
N_BUCKETS = 32
MAX_DISTANCE = 128
N_GROUPS = 4
EXPERTS_PER_GROUP = 4
N_EXPERTS = N_GROUPS * EXPERTS_PER_GROUP
D_EXPERT = 256
EPS = 1e-6
IN_TOTAL = 7 * 512
LAM_INIT = 0.8 - 0.6 * math.exp(-0.3 * 0)

NEG_BIG = -1e30
V7X_LANES = 128
VMEM_LIMIT = 56 * 1024 * 1024

def _t5_bucket_py(n):
    max_exact = N_BUCKETS // 2
    if n < max_exact:
        return n
    return min(max_exact + int(math.log(n / max_exact) / math.log(MAX_DISTANCE / max_exact)
                               * (N_BUCKETS - max_exact)), N_BUCKETS - 1)


LAST_BUCKET_START = next(n for n in range(MAX_DISTANCE + 1) if _t5_bucket_py(n) == N_BUCKETS - 1)

ATT_BLOCK = 512
SUM_ROWS = 16
HG_CHUNK = 128
HG_SUBCHUNKS = 4
HG_DIAG = 4
SAMPLE_CHUNK = 16
MOE_SLAB = 128
PAGES_PER_STEP = 32
SAMPLE_STREAMS = 1


def _cparams(*sem):
    return pltpu.CompilerParams(dimension_semantics=sem, vmem_limit_bytes=VMEM_LIMIT)


def _dot(a, b):
    return jnp.dot(a, b, preferred_element_type=F32)


def _dot_nt(a, b):
    return lax.dot_general(a, b, (((1,), (1,)), ((), ())), preferred_element_type=F32)


def _dot_tn(a, b):
    return lax.dot_general(a, b, (((0,), (0,)), ((), ())), preferred_element_type=F32)


def _rms(x, w):
    return x * lax.rsqrt(jnp.mean(x * x, axis=-1, keepdims=True) + EPS) * w


def _bias_body(rb_ref, n_ref, o_ref, *, interleaved):
    h = pl.program_id(0)
    n = n_ref[...]
    if interleaved:
        lane = lax.broadcasted_iota(jnp.int32, n.shape, 1)
        n = jnp.where(lane % DA_HEADS == h, n, -1)
    nn = jnp.maximum(n, 0)
    max_exact = N_BUCKETS // 2
    nf = jnp.maximum(nn, max_exact).astype(F32)
    large = max_exact + jnp.floor(jnp.log(nf / max_exact) / math.log(MAX_DISTANCE / max_exact)
                                  * (N_BUCKETS - max_exact)).astype(jnp.int32)
    large = jnp.minimum(large, N_BUCKETS - 1)
    bucket = jnp.where(nn < max_exact, nn, large)
    bias = jnp.zeros(n.shape, F32)
    for b in range(N_BUCKETS):
        bias = jnp.where(bucket == b, rb_ref[b, h], bias)
    o_ref[0] = jnp.where(n >= 0, bias, NEG_BIG)


def _bias_tiles(rel_bias, n, interleaved):
    r, c = n.shape
    return pl.pallas_call(
        functools.partial(_bias_body, interleaved=interleaved),
        grid=(DA_HEADS,),
        in_specs=[pl.BlockSpec(memory_space=pltpu.SMEM),
                  pl.BlockSpec((r, c), lambda h: (0, 0))],
        out_specs=pl.BlockSpec((1, r, c), lambda h: (h, 0, 0)),
        out_shape=jax.ShapeDtypeStruct((DA_HEADS, r, c), F32),
        compiler_params=_cparams("arbitrary"),
        name="bias_tiles",
    )(rel_bias, n)


def _proj_body(x_ref, nw_ref, w_ref, q_ref, k_ref, v_ref, kb_ref, vb_ref, hg_ref):
    hb = _rms(x_ref[...], nw_ref[...]).astype(BF16)
    q_ref[...] = (_dot(hb, w_ref[:, 0:512]) * (DA_QK ** -0.5)).astype(BF16)
    tm = x_ref.shape[0]
    k = _dot(hb, w_ref[:, 512:1024])
    kb_ref[...] = k.astype(BF16)
    v = _dot(hb, w_ref[:, 1024:1536])
    vb_ref[...] = v.astype(BF16)
    for h in range(DA_HEADS):
        k_ref[pl.ds(h, tm, stride=DA_HEADS), :] = k[:, h * DA_V:(h + 1) * DA_V]
        v_ref[pl.ds(h, tm, stride=DA_HEADS), :] = v[:, h * DA_V:(h + 1) * DA_V]
    hg_ref[...] = _dot(hb, w_ref[:, 1536:IN_TOTAL])


def _proj(x, norm_w, w_in_b, tm):
    t = x.shape[0]
    row = lambda i: (i, 0)
    const = lambda i: (0, 0)
    return pl.pallas_call(
        _proj_body,
        grid=(t // tm,),
        in_specs=[pl.BlockSpec((tm, D_MODEL), row),
                  pl.BlockSpec((1, D_MODEL), const),
                  pl.BlockSpec((D_MODEL, IN_TOTAL), const)],
        out_specs=[pl.BlockSpec((tm, 512), row),
                   pl.BlockSpec((tm * DA_HEADS, DA_V), row),
                   pl.BlockSpec((tm * DA_HEADS, DA_V), row),
                   pl.BlockSpec((tm, 512), row),
                   pl.BlockSpec((tm, 512), row),
                   pl.BlockSpec((tm, 2048), row)],
        out_shape=[jax.ShapeDtypeStruct((t, 512), BF16),
                   jax.ShapeDtypeStruct((t * DA_HEADS, DA_V), F32),
                   jax.ShapeDtypeStruct((t * DA_HEADS, DA_V), F32),
                   jax.ShapeDtypeStruct((t, 512), BF16),
                   jax.ShapeDtypeStruct((t, 512), BF16),
                   jax.ShapeDtypeStruct((t, 2048), F32)],
        compiler_params=_cparams("arbitrary"),
        name="proj",
    )(x, norm_w, w_in_b)


def _softmax_update(s, v, m_ref, l_ref, acc_ref, idx):
    m_prev = m_ref[idx]
    m_new = jnp.maximum(m_prev, jnp.max(s, axis=-1, keepdims=True))
    alpha = jnp.exp(m_prev - m_new)
    p = jnp.exp(s - m_new)
    l_ref[idx] = alpha * l_ref[idx] + jnp.sum(p, axis=-1, keepdims=True)
    acc_ref[idx] = alpha * acc_ref[idx] + _dot(p.astype(BF16), v)
    m_ref[idx] = m_new


def _sub_ln(o, w):
    return _rms(o, w) * (1.0 - LAM_INIT)


def _attn_prompt_body(lam_ref, rb_ref, q_ref, k_ref, v_ref, bias_ref, w_ref, o_ref,
                      vt_ref, m_ref, acc_ref, s_ref):
    h = pl.program_id(1)
    qi = pl.program_id(2)
    tb = ATT_BLOCK
    nblk = vt_ref.shape[0]

    @pl.when(qi == 0)
    def _():
        for j in range(nblk):
            vt_ref[j, 0:DA_V, :] = v_ref[j * tb:(j + 1) * tb, :].T
            vt_ref[j, DA_V:DA_V + SUM_ROWS, :] = jnp.ones((SUM_ROWS, tb), BF16)

    q = q_ref[...]
    lane = lax.broadcasted_iota(jnp.int32, q.shape, 1)
    zero = jnp.zeros_like(q)
    q2 = jnp.concatenate([jnp.where(lane < DA_QK, q, zero), jnp.where(lane >= DA_QK, q, zero)],
                         axis=0)

    m_ref[...] = jnp.full(m_ref.shape, NEG_BIG, F32)
    acc_ref[...] = jnp.zeros(acc_ref.shape, F32)

    def scores(first_blk, nblocks):
        start = first_blk * tb
        if not isinstance(start, int):
            start = pl.multiple_of(start, tb)
        return _dot_nt(k_ref[pl.ds(start, nblocks * tb), :], q2)

    def fold(s, first_blk, nblocks, bias, far_bias):
        m_prev = m_ref[...]
        if far_bias is None:
            s = s + jnp.concatenate([bias, bias], axis=1)
            m_new = jnp.maximum(m_prev, jnp.max(s, axis=0, keepdims=True))
            shift = m_new
        else:
            m_new = jnp.maximum(m_prev, jnp.max(s, axis=0, keepdims=True) + far_bias)
            shift = m_new - far_bias
        p = jnp.exp(s - shift).astype(BF16)
        pv = _dot(vt_ref[first_blk], p[0:tb])
        for i in range(1, nblocks):
            pv = pv + _dot(vt_ref[first_blk + i], p[i * tb:(i + 1) * tb])
        acc_ref[...] = jnp.exp(m_prev - m_new) * acc_ref[...] + pv
        m_ref[...] = m_new

    far_bias = rb_ref[N_BUCKETS - 1, h]
    n_far = jnp.maximum(qi - 1, 0)
    odd = n_far % 2


    last = n_far // 2
    pair_blk = lambda p: odd + 2 * p

    def near_fold(slot):
        off = pl.multiple_of(jnp.where(qi == 0, tb, 0), tb)
        fold(s_ref[slot], pair_blk(last), 2, bias_ref[0, pl.ds(off, 2 * tb), :], None)

    @pl.when(odd == 0)
    def _():
        s_ref[0] = scores(pair_blk(0), 2)

    @pl.when(odd == 1)
    def _():
        s_single = scores(0, 1)
        s_ref[0] = scores(1, 2)
        fold(s_single, 0, 1, None, far_bias)

    def two_far_pairs(t, carry):
        s_ref[1] = scores(pair_blk(2 * t + 1), 2)
        fold(s_ref[0], pair_blk(2 * t), 2, None, far_bias)
        s_ref[0] = scores(pair_blk(2 * t + 2), 2)
        fold(s_ref[1], pair_blk(2 * t + 1), 2, None, far_bias)
        return carry

    lax.fori_loop(0, last // 2, two_far_pairs, 0)

    @pl.when(last % 2 == 0)
    def _():
        near_fold(0)

    @pl.when(last % 2 == 1)
    def _():
        s_ref[1] = scores(pair_blk(last), 2)
        fold(s_ref[0], pair_blk(last - 1), 2, None, far_bias)
        near_fold(1)

    a0 = acc_ref[:, 0:tb]
    a1 = acc_ref[:, tb:2 * tb]
    o = (a0[0:DA_V] / a0[DA_V:DA_V + 1]
         - lam_ref[0] * (a1[0:DA_V] / a1[DA_V:DA_V + 1]))
    o = o * lax.rsqrt(jnp.mean(o * o, axis=0, keepdims=True) + EPS) * w_ref[...]
    o_ref[...] = (o * (1.0 - LAM_INIT)).T.astype(BF16)


def _attn_prompt(lam, rel_bias, qb, kb, vb, bias, subln_col, batch, seq):
    tb = ATT_BLOCK
    assert tb + 1 >= LAST_BUCKET_START, "far blocks must lie entirely in the last T5 bucket"
    nq = seq // tb
    return pl.pallas_call(
        _attn_prompt_body,
        grid=(batch, DA_HEADS, nq),
        in_specs=[pl.BlockSpec(memory_space=pltpu.SMEM),
                  pl.BlockSpec(memory_space=pltpu.SMEM),
                  pl.BlockSpec((tb, DA_V), lambda b, h, i: (b * nq + i, h)),
                  pl.BlockSpec((seq, DA_V), lambda b, h, i: (b, h)),
                  pl.BlockSpec((seq, DA_V), lambda b, h, i: (b, h)),
                  pl.BlockSpec((1, 3 * tb, tb), lambda b, h, i: (h, 0, 0)),
                  pl.BlockSpec((DA_V, 1), lambda b, h, i: (0, 0))],
        out_specs=pl.BlockSpec((tb, DA_V), lambda b, h, i: (b * nq + i, h)),
        out_shape=jax.ShapeDtypeStruct((batch * seq, DA_W), BF16),
        scratch_shapes=[pltpu.VMEM((nq, DA_V + SUM_ROWS, tb), BF16),
                        pltpu.VMEM((1, 2 * tb), F32),
                        pltpu.VMEM((DA_V + SUM_ROWS, 2 * tb), F32),
                        pltpu.VMEM((2, 2 * tb, 2 * tb), F32)],
        compiler_params=_cparams("arbitrary", "arbitrary", "arbitrary"),
        name="attn_prompt",
    )(lam, rel_bias, qb, kb, vb, bias, subln_col)


def _attn_sample_body(pt_ref, lam_ref, q_ref, bias_ref, kn_ref, vn_ref, w_ref, *rest):
    npg = PAGES_PER_STEP
    k_refs = rest[:npg]
    v_refs = rest[npg:2 * npg]
    o_ref = rest[2 * npg]
    m_ref, l_ref, acc_ref = rest[2 * npg + 1:]
    j = pl.program_id(1)
    last = pl.num_programs(1) - 1
    rows = DA_HEADS * 8

    @pl.when(j == 0)
    def _():
        m_ref[...] = jnp.full(m_ref.shape, NEG_BIG, F32)
        l_ref[...] = jnp.zeros(l_ref.shape, F32)
        acc_ref[...] = jnp.zeros(acc_ref.shape, F32)

    q = q_ref[0]
    lane = lax.broadcasted_iota(jnp.int32, q.shape, 1)
    row = lax.broadcasted_iota(jnp.int32, q.shape, 0)
    w = jnp.where(lane // DA_QK == (row // 4) % 2, q, jnp.zeros_like(q))

    far = bias_ref[0]
    near = bias_ref[1]
    per = npg // SAMPLE_STREAMS
    for g in range(SAMPLE_STREAMS):
        pages = range(g * per, (g + 1) * per)
        kcat = jnp.concatenate([k_refs[i][0].astype(BF16) for i in pages], axis=0)
        bias = jnp.concatenate(
            [far if i < npg - 1 else jnp.where(j == last, near, far) for i in pages], axis=1)
        s = _dot_nt(w, kcat) + bias
        vcat = jnp.concatenate([v_refs[i][0].astype(BF16) for i in pages], axis=0)
        _softmax_update(s, vcat, m_ref, l_ref, acc_ref, g)

    @pl.when(j == last)
    def _():
        s_new = _dot_nt(w, kn_ref[0]) + bias_ref[2][:, 0:rows]
        _softmax_update(s_new, vn_ref[0], m_ref, l_ref, acc_ref, 0)
        m_all = m_ref[0]
        for g in range(1, SAMPLE_STREAMS):
            m_all = jnp.maximum(m_all, m_ref[g])
        l_all = jnp.zeros_like(m_all)
        acc_all = jnp.zeros(acc_ref.shape[1:], F32)
        for g in range(SAMPLE_STREAMS):
            scale = jnp.exp(m_ref[g] - m_all)
            l_all = l_all + scale * l_ref[g]
            acc_all = acc_all + scale * acc_ref[g]
        normed = acc_all / l_all
        for h in range(DA_HEADS):
            blk = normed[h * 8:(h + 1) * 8, :]
            o = blk[0:4] - lam_ref[0] * blk[4:8]
            o_ref[0, :, h * DA_V:(h + 1) * DA_V] = _sub_ln(o, w_ref[...]).astype(BF16)


def _attn_sample(page_table, lam, q_rows, bias, k_new, v_new, subln_w, cache_k, cache_v):
    nb, n_pages = page_table.shape
    npg = PAGES_PER_STEP
    assert n_pages % npg == 0, (n_pages, npg)
    rows = DA_HEADS * 8
    page_rows = PAGE_SIZE * DA_HEADS

    def page_spec(i):
        return pl.BlockSpec((1, page_rows, DA_V),
                            lambda b, j, pt: (pt[b, j * npg + i], 0, 0))

    seq_spec = lambda shape: pl.BlockSpec(shape, lambda b, j, pt: (b, 0, 0))
    grid_spec = pltpu.PrefetchScalarGridSpec(
        num_scalar_prefetch=1,
        grid=(nb, n_pages // npg),
        in_specs=[pl.BlockSpec(memory_space=pltpu.SMEM),
                  seq_spec((1, rows, DA_V)),
                  pl.BlockSpec((3, rows, page_rows), lambda b, j, pt: (0, 0, 0)),
                  seq_spec((1, rows, DA_V)),
                  seq_spec((1, rows, DA_V)),
                  pl.BlockSpec((1, DA_V), lambda b, j, pt: (0, 0))]
                 + [page_spec(i) for i in range(npg)] * 2,
        out_specs=seq_spec((1, 4, DA_W)),
        scratch_shapes=[pltpu.VMEM((SAMPLE_STREAMS, rows, 1), F32),
                        pltpu.VMEM((SAMPLE_STREAMS, rows, 1), F32),
                        pltpu.VMEM((SAMPLE_STREAMS, rows, DA_V), F32)],
    )
    return pl.pallas_call(
        _attn_sample_body,
        grid_spec=grid_spec,
        out_shape=jax.ShapeDtypeStruct((nb, 4, DA_W), BF16),
        compiler_params=_cparams("arbitrary", "arbitrary"),
        name="attn_sample",
    )(page_table, lam, q_rows, bias, k_new, v_new, subln_w,
      *([cache_k] * npg), *([cache_v] * npg))


def _split3(x):
    hi = x.astype(BF16)
    r = x - hi.astype(F32)
    mid = r.astype(BF16)
    lo = (r - mid.astype(F32)).astype(BF16)
    return jnp.concatenate([hi, mid, lo], axis=1)


def _merge3(p):
    w = p.shape[1] // 3
    return p[:, 0:w] + p[:, w:2 * w] + p[:, 2 * w:3 * w]


def _hgrn_chunks(hg_ref, lb_ref, nw_ref, st_ref, o_ref, chunk, nsub, diag, valid):
    c = chunk
    units = [(s, h) for s in range(nsub) for h in range(HG_HEADS)]
    part = lambda p, u: hg_ref[u[0] * c:(u[0] + 1) * c,
                               p * HG_W + u[1] * HG_K: p * HG_W + (u[1] + 1) * HG_K]
    row = lax.broadcasted_iota(jnp.int32, (c, HG_K), 0)
    r2 = lax.broadcasted_iota(jnp.int32, (c, c), 0)
    c2 = lax.broadcasted_iota(jnp.int32, (c, c), 1)
    tri = jnp.where(c2 <= r2, 1.0, 0.0).astype(BF16)

    q, kk, vb, gc = {}, {}, {}, {}
    for u in units:
        hq = part(0, u)
        lb = lb_ref[:, u[1] * HG_K:(u[1] + 1) * HG_K]
        qu = hq * jax.nn.sigmoid(hq) * (HG_K ** -0.5)
        f = lb + (1.0 - lb) * jax.nn.sigmoid(part(1, u))
        if valid < c:
            f = jnp.where(row < valid, f, 1.0)
            qu = jnp.where(row < valid, qu, 0.0)
        q[u] = qu
        kk[u] = 1.0 - f
        vb[u] = part(2, u).astype(BF16)
        gc[u] = _merge3(_dot(tri, _split3(jnp.log2(f))))

    packed = c == V7X_LANES
    a = {u: jnp.zeros((c, c), F32) for u in units}
    rmod = row % diag
    for d in range(diag):
        for u in units:
            if d == 0:
                x = q[u] * kk[u]
            else:
                e = jnp.where(rmod >= d, gc[u] - pltpu.roll(gc[u], d, axis=0), NEG_BIG)
                x = q[u] * pltpu.roll(kk[u], d, axis=0) * jnp.exp2(e)
            xsum = jnp.sum(x, axis=-1, keepdims=True)
            if packed:
                a[u] = jnp.where(c2 == (c - d) % c, xsum, a[u])
            else:
                a[u] = a[u] + jnp.where(r2 - c2 == d, xsum, 0.0)
    if packed:
        for u in units:
            a[u] = pltpu.roll(a[u], 0, axis=1, stride=1, stride_axis=0)

    levels = []
    half = diag
    while half < c:
        levels.append(half)
        half *= 2
    def boundary(g, half):
        reps = [jnp.broadcast_to(g[b + half - 1:b + half, :], (2 * half, HG_K))
                for b in range(0, c, 2 * half)]
        return reps[0] if len(reps) == 1 else jnp.concatenate(reps, axis=0)

    if levels:
        for half in levels:
            second = (row % (2 * half)) >= half
            same = r2 // (2 * half) == c2 // (2 * half)
            for u in units:
                bnd = boundary(gc[u], half)
                qh = q[u] * jnp.exp2(jnp.where(second, gc[u] - bnd, NEG_BIG))
                kh = kk[u] * jnp.exp2(jnp.where(second, NEG_BIG, bnd - gc[u]))
                al = _dot_nt(qh.astype(BF16), kh.astype(BF16))
                a[u] = a[u] + jnp.where(same, al, 0.0)

    for u in units:
        s, h = u
        st = st_ref[h]
        g_last = gc[u][c - 1:c, :]
        o = (_dot_nt((q[u] * jnp.exp2(gc[u])).astype(BF16), st.astype(BF16))
             + _dot(a[u].astype(BF16), vb[u]))
        kl = (kk[u] * jnp.exp2(g_last - gc[u])).astype(BF16)
        st_ref[h] = st * jnp.exp2(g_last) + _dot_tn(vb[u], kl)
        hgate = part(3, u)
        o = _rms(o, nw_ref[...]) * (hgate * jax.nn.sigmoid(hgate))
        o_ref[s * c:(s + 1) * c, h * HG_V:(h + 1) * HG_V] = o.astype(BF16)


def _hgrn_body(hg_ref, lb_ref, nw_ref, s0_ref, o_ref, sf_ref, st_ref, *, chunk, nsub, diag, valid):
    ci = pl.program_id(1)

    @pl.when(ci == 0)
    def _():
        for h in range(HG_HEADS):
            st_ref[h] = s0_ref[0, h].T

    _hgrn_chunks(hg_ref, lb_ref, nw_ref, st_ref, o_ref, chunk, nsub, diag, valid)

    @pl.when(ci == pl.num_programs(1) - 1)
    def _():
        for h in range(HG_HEADS):
            sf_ref[0, h] = st_ref[h].T


def _hgrn(hg, lb, norm_w, s0, batch, length, chunk, nsub, diag, valid):
    rows = chunk * nsub
    nc = length // rows
    body = functools.partial(_hgrn_body, chunk=chunk, nsub=nsub, diag=diag, valid=valid)
    state_spec = pl.BlockSpec((1, HG_HEADS, HG_K, HG_V), lambda b, c: (b, 0, 0, 0))
    return pl.pallas_call(
        body,
        grid=(batch, nc),
        in_specs=[pl.BlockSpec((rows, 4 * HG_W), lambda b, c: (b * nc + c, 0)),
                  pl.BlockSpec((1, HG_W), lambda b, c: (0, 0)),
                  pl.BlockSpec((1, HG_V), lambda b, c: (0, 0)),
                  state_spec],
        out_specs=[pl.BlockSpec((rows, HG_W), lambda b, c: (b * nc + c, 0)), state_spec],
        out_shape=[jax.ShapeDtypeStruct((batch * length, HG_W), BF16),
                   jax.ShapeDtypeStruct((batch, HG_HEADS, HG_K, HG_V), F32)],
        scratch_shapes=[pltpu.VMEM((HG_HEADS, HG_V, HG_K), F32)],
        compiler_params=_cparams("arbitrary", "arbitrary"),
        name="hgrn",
    )(hg, lb, norm_w, s0)


def _route(logits):
    lane = lax.broadcasted_iota(jnp.int32, logits.shape, 1)
    neg = jnp.float32(-jnp.inf)

    def top(mask):
        masked = jnp.where(mask, logits, neg)
        idx = jnp.argmax(masked, axis=-1, keepdims=True).astype(jnp.int32)
        return jnp.max(masked, axis=-1, keepdims=True), idx

    gmask = lane < N_GROUPS
    gmax, gidx = top(gmask)
    p_top = 1.0 / jnp.sum(jnp.where(gmask, jnp.exp(logits - gmax), 0.0), axis=-1, keepdims=True)
    lo = N_GROUPS + gidx * EXPERTS_PER_GROUP
    emask = (lane >= lo) & (lane < lo + EXPERTS_PER_GROUP)
    t1, i1 = top(emask)
    t2, i2 = top(emask & (lane != i1))
    e2 = jnp.exp(t2 - t1)
    g1 = p_top / (1.0 + e2)
    g2 = p_top * e2 / (1.0 + e2)
    return gidx, i1 - N_GROUPS, i2 - N_GROUPS, g1, g2


def _mix_moe_body(x_ref, oda_ref, ohg_ref, wo_ref, n2_ref, wr_ref, br_ref,
                  w1_ref, w3_ref, w2_ref, fw_ref, y_ref, xs_ref, cws_ref, ys_ref):
    tm = x_ref.shape[0]
    x1 = (x_ref[...] + _dot(oda_ref[...], wo_ref[0:DA_W, :])
          + _dot(ohg_ref[...], wo_ref[DA_W:DA_W + HG_W, :]))
    h2 = _rms(x1, n2_ref[...])
    hb = h2.astype(BF16)
    hl = (h2 - hb.astype(F32)).astype(BF16)
    hw = _dot(hb, wr_ref[...])
    logits = (hw[:, 0:V7X_LANES] + hw[:, V7X_LANES:2 * V7X_LANES]
              + _dot(hl, wr_ref[:, 0:V7X_LANES])) + br_ref[...]
    gidx, id1, id2, g1, g2 = _route(logits)

    lane = lax.broadcasted_iota(jnp.int32, logits.shape, 1)
    base = gidx * EXPERTS_PER_GROUP
    cw = jnp.where(lane == id1 - base, g1, 0.0) + jnp.where(lane == id2 - base, g2, 0.0)

    gone = jnp.where(lane == gidx, 1.0, 0.0)
    r2 = lax.broadcasted_iota(jnp.int32, (tm, tm), 0)
    c2 = lax.broadcasted_iota(jnp.int32, (tm, tm), 1)
    before = _dot(jnp.where(c2 < r2, 1.0, 0.0).astype(BF16), gone.astype(BF16))
    sizes_row = before[tm - 1:tm] + gone[tm - 1:tm]
    lane1 = lax.broadcasted_iota(jnp.int32, (1, V7X_LANES), 1)
    sizes = [jnp.sum(jnp.where(lane1 == g, sizes_row, 0.0)).astype(jnp.int32)
             for g in range(N_GROUPS)]
    starts = [jnp.int32(0)]
    for g in range(1, N_GROUPS):
        starts.append(starts[-1] + sizes[g - 1])
    rank = jnp.sum(jnp.where(lane == gidx, before, 0.0), axis=-1, keepdims=True).astype(jnp.int32)
    pos = rank
    for g in range(1, N_GROUPS):
        pos = pos + jnp.where(gidx == g, starts[g], 0)
    perm = jnp.where(c2 == pos, 1.0, 0.0).astype(BF16)

    xs_ref[...] = _dot_tn(perm, hb).astype(BF16)
    cw_hi = cw.astype(BF16)
    cw_lo = (cw - cw_hi.astype(F32)).astype(BF16)
    cws = _dot_tn(perm, jnp.concatenate([cw_hi, cw_lo], axis=1))
    cws_ref[...] = cws[:, 0:V7X_LANES] + cws[:, V7X_LANES:2 * V7X_LANES]
    ys_ref[...] = jnp.zeros(ys_ref.shape, F32)

    rowi = lax.broadcasted_iota(jnp.int32, (MOE_SLAB, 1), 0)
    group_w = EXPERTS_PER_GROUP * D_EXPERT
    for g in range(N_GROUPS):
        lo = starts[g]
        hi = starts[g] + sizes[g]

        def slab(sb, carry, g=g, lo=lo, hi=hi):
            r0 = pl.multiple_of(sb * MOE_SLAB, MOE_SLAB)
            rows = xs_ref[pl.ds(r0, MOE_SLAB), :]
            mine = (r0 + rowi >= lo) & (r0 + rowi < hi)
            cwb = cws_ref[pl.ds(r0, MOE_SLAB), :]
            parts = []
            for j in range(EXPERTS_PER_GROUP):
                e = g * EXPERTS_PER_GROUP + j
                he = _dot(rows, w1_ref[e])
                he = he * jax.nn.sigmoid(he) * _dot(rows, w3_ref[e])
                parts.append((jnp.where(mine, cwb[:, j:j + 1], 0.0) * he).astype(BF16))
            ys_ref[pl.ds(r0, MOE_SLAB), :] += _dot(jnp.concatenate(parts, axis=1),
                                                   w2_ref[g * group_w:(g + 1) * group_w, :])
            return carry

        lax.fori_loop(lo // MOE_SLAB, (hi + MOE_SLAB - 1) // MOE_SLAB, slab, 0)

    y = _dot(perm, ys_ref[...].astype(BF16))
    y_ref[...] = _rms(x1 + y, fw_ref[...])


def _mix_moe(x, oda, ohg, wo_b, norm2_w, wr, br, w1_b, w3_b, w2_b, final_w, tm):
    t = x.shape[0]
    row = lambda i: (i, 0)
    c2 = lambda i: (0, 0)
    c3 = lambda i: (0, 0, 0)
    once = pl.Buffered(1)
    return pl.pallas_call(
        _mix_moe_body,
        grid=(t // tm,),
        in_specs=[pl.BlockSpec((tm, D_MODEL), row),
                  pl.BlockSpec((tm, DA_W), row),
                  pl.BlockSpec((tm, HG_W), row),
                  pl.BlockSpec((D_MODEL, D_MODEL), c2, pipeline_mode=once),
                  pl.BlockSpec((1, D_MODEL), c2),
                  pl.BlockSpec((D_MODEL, 2 * V7X_LANES), c2, pipeline_mode=once),
                  pl.BlockSpec((1, V7X_LANES), c2),
                  pl.BlockSpec((N_EXPERTS, D_MODEL, D_EXPERT), c3, pipeline_mode=once),
                  pl.BlockSpec((N_EXPERTS, D_MODEL, D_EXPERT), c3, pipeline_mode=once),
                  pl.BlockSpec((N_EXPERTS * D_EXPERT, D_MODEL), c2, pipeline_mode=once),
                  pl.BlockSpec((1, D_MODEL), c2)],
        out_specs=pl.BlockSpec((tm, D_MODEL), row),
        out_shape=jax.ShapeDtypeStruct((t, D_MODEL), F32),
        scratch_shapes=[pltpu.VMEM((tm, D_MODEL), BF16),
                        pltpu.VMEM((tm, V7X_LANES), F32),
                        pltpu.VMEM((tm, D_MODEL), F32)],
        compiler_params=_cparams("arbitrary"),
        name="mix_moe",
    )(x, oda, ohg, wo_b, norm2_w, wr, br, w1_b, w3_b, w2_b, final_w)


def _prompt_distances():
    tb = ATT_BLOCK
    key = jnp.arange(tb, dtype=jnp.int32)[:, None]
    qry = jnp.arange(tb, dtype=jnp.int32)[None, :]
    return jnp.concatenate([tb + qry - key, qry - key, qry - key - tb], axis=0)


def _sample_distances(dec_seq):
    t = (jnp.arange(8, dtype=jnp.int32) % 4)[:, None]
    key = (jnp.arange(PAGE_SIZE * DA_HEADS, dtype=jnp.int32) // DA_HEADS)[None, :]
    far = jnp.full((8, PAGE_SIZE * DA_HEADS), 1 << 20, jnp.int32)
    near = PAGE_SIZE + t - key
    new = jnp.where(key < dec_seq, t - key, -1)
    return jnp.concatenate([far, near, new], axis=0)


def kernel(x_prompt, x_sample, cache_k, cache_v, state_hgrn, page_table, norm1_w, w_in, rel_bias,
           lambda_q1, lambda_k1, lambda_q2, lambda_k2, da_subln_w, hg_lb_logits, hg_norm_w, w_out,
           norm2_w, router_g_w, router_g_b, router_e_w, router_e_b, expert_w1, expert_w3,
           expert_w2, final_norm_w):
    batch, seq, _ = x_prompt.shape
    nb, dec_seq, _ = x_sample.shape
    assert dec_seq == 4 and seq % ATT_BLOCK == 0 and seq % HG_CHUNK == 0
    l = 0

    lam = (jnp.exp(jnp.sum(lambda_q1[l].astype(F32) * lambda_k1[l].astype(F32)))
           - jnp.exp(jnp.sum(lambda_q2[l].astype(F32) * lambda_k2[l].astype(F32))) + LAM_INIT)
    lam = lam.reshape(1)
    lb = jnp.cumsum(jax.nn.softmax(hg_lb_logits.astype(F32), axis=0), axis=0)[l].reshape(1, HG_W)
    w_in_b = w_in[l].astype(BF16)
    wo_b = w_out[l].astype(BF16)
    w1_b = expert_w1[l].astype(BF16)
    w3_b = expert_w3[l].astype(BF16)
    w2_b = expert_w2[l].astype(BF16).reshape(N_EXPERTS * D_EXPERT, D_MODEL)
    pad = V7X_LANES - N_GROUPS - N_EXPERTS
    wr_f = jnp.pad(jnp.concatenate([router_g_w[l], router_e_w[l]], axis=1), ((0, 0), (0, pad)))
    wr_hi = wr_f.astype(BF16)
    wr = jnp.concatenate([wr_hi, (wr_f - wr_hi.astype(F32)).astype(BF16)], axis=1)
    br = jnp.pad(jnp.concatenate([router_g_b[l], router_e_b[l]]), (0, pad)).reshape(1, V7X_LANES)
    n1 = norm1_w[l].reshape(1, D_MODEL)
    n2 = norm2_w[l].reshape(1, D_MODEL)
    fw = final_norm_w.reshape(1, D_MODEL)
    subln = da_subln_w[l].reshape(1, DA_V)
    hgn = hg_norm_w[l].reshape(1, HG_V)

    tp = batch * seq
    qb, k_p, v_p, kb, vb, hg = _proj(x_prompt.reshape(tp, D_MODEL), n1, w_in_b, 512)
    bias_p = _bias_tiles(rel_bias, _prompt_distances(), False)
    oda = _attn_prompt(lam, rel_bias, qb, kb, vb, bias_p, subln.reshape(DA_V, 1), batch, seq)
    s0p = jnp.zeros((batch, HG_HEADS, HG_K, HG_V), F32)
    ohg, s_p = _hgrn(hg, lb, hgn, s0p, batch, seq, HG_CHUNK, HG_SUBCHUNKS, HG_DIAG, HG_CHUNK)
    y_p = _mix_moe(x_prompt.reshape(tp, D_MODEL), oda, ohg, wo_b, n2, wr, br,
                   w1_b, w3_b, w2_b, fw, 512)

    ts = nb * dec_seq
    qs, k_s, v_s, ksb, vsb, hgs = _proj(x_sample.reshape(ts, D_MODEL), n1, w_in_b, ts)
    page_rows = PAGE_SIZE * DA_HEADS
    bias_s = _bias_tiles(rel_bias, _sample_distances(dec_seq), True)
    bias_s = bias_s.reshape(DA_HEADS, 3, 8, page_rows).transpose(1, 0, 2, 3).reshape(
        3, DA_HEADS * 8, page_rows)
    q_rows = jnp.broadcast_to(
        qs.reshape(nb, dec_seq, DA_HEADS, 1, DA_V).transpose(0, 2, 3, 1, 4),
        (nb, DA_HEADS, 2, dec_seq, DA_V)).reshape(nb, DA_HEADS * 8, DA_V)
    new_rows = lambda a: jnp.pad(a.reshape(nb, dec_seq * DA_HEADS, DA_V),
                                 ((0, 0), (0, (8 - dec_seq) * DA_HEADS), (0, 0)))
    n_pool = cache_k.shape[1]
    oda_s = _attn_sample(page_table, lam, q_rows, bias_s, new_rows(ksb), new_rows(vsb), subln,
                         cache_k[l].reshape(n_pool, page_rows, DA_V),
                         cache_v[l].reshape(n_pool, page_rows, DA_V))
    hgs_pad = jnp.pad(hgs.reshape(nb, dec_seq, 4 * HG_W),
                      ((0, 0), (0, SAMPLE_CHUNK - dec_seq), (0, 0)))
    ohg_s, s_s = _hgrn(hgs_pad.reshape(nb * SAMPLE_CHUNK, 4 * HG_W), lb, hgn, state_hgrn[l],
                       nb, SAMPLE_CHUNK, SAMPLE_CHUNK, 1, HG_DIAG, dec_seq)
    ohg_s = ohg_s.reshape(nb, SAMPLE_CHUNK, HG_W)[:, :dec_seq].reshape(ts, HG_W)
    y_s = _mix_moe(x_sample.reshape(ts, D_MODEL), oda_s.reshape(ts, DA_W), ohg_s, wo_b, n2, wr, br,
                   w1_b, w3_b, w2_b, fw, ts)

    return (y_p.reshape(batch, seq, D_MODEL),
            y_s.reshape(nb, dec_seq, D_MODEL),
            k_p.reshape(1, batch, seq, DA_HEADS, 2 * DA_QK),
            v_p.reshape(1, batch, seq, DA_HEADS, DA_V),
            s_p.reshape(1, batch, HG_HEADS, HG_K, HG_V),
            k_s.reshape(1, nb, dec_seq, DA_HEADS, 2 * DA_QK),
            v_s.reshape(1, nb, dec_seq, DA_HEADS, DA_V),
            s_s.reshape(1, nb, HG_HEADS, HG_K, HG_V))
```
